```python
import math
import jax, jax.numpy as jnp
from jax import lax
import numpy as np

D_MODEL = 2048
BATCH = 4
SEQ = 2048
DEPTH = 4
DEC_BATCH = 8
DEC_SEQ = 1
PAST_LEN = 16384
PAGE_SIZE = 128

HEAD_DIM = 128
ATT_GROUPS = ((128, 1), (512, 4), (2048, 16))
ATT_HEADS_PER_GROUP = 4
N_ATT_HEADS = len(ATT_GROUPS) * ATT_HEADS_PER_GROUP
ATT_WIDTH = N_ATT_HEADS * HEAD_DIM
ATT_OUT_WIDTH = ATT_HEADS_PER_GROUP * HEAD_DIM
ATT_BLOCK = 128
ROPE_THETA = 10000.0
M_HEADS = 4
M_WIDTH = D_MODEL // 2
M_HEAD_DIM = M_WIDTH // M_HEADS
M_CHUNK = 64
N_EXPERTS = 16
N_EXPERT_GROUPS = 4
EXPERTS_PER_GROUP = N_EXPERTS // N_EXPERT_GROUPS
TOP_K = 2
D_EXPERT = D_MODEL // 4
MOE_BLOCK = 128
EPS = 1e-6
NEG_INF = -1e30
IN_SPLITS = (ATT_WIDTH, ATT_WIDTH, ATT_WIDTH, M_WIDTH, M_WIDTH, M_WIDTH, M_HEADS, M_HEADS, M_WIDTH, D_MODEL, D_MODEL)
D_IN = sum(IN_SPLITS)
IN_OFFSETS = tuple(int(v) for v in np.cumsum(IN_SPLITS)[:-1])

kernel_name = "dilated_mlstm_gated_hybrid_moe_step"


def rmsnorm(x, g):
    xf = x.astype(jnp.float32)
    y = xf * lax.rsqrt(jnp.mean(xf * xf, axis=-1, keepdims=True) + EPS)
    return (y * g.astype(jnp.float32)).astype(x.dtype)


def rotary(x, pos):
    half = HEAD_DIM // 2
    inv = ROPE_THETA ** (-jnp.arange(half, dtype=jnp.float32) / half)
    ang = pos.astype(jnp.float32)[:, None] * inv[None, :]
    cos = jnp.cos(ang)[None, :, None, :]
    sin = jnp.sin(ang)[None, :, None, :]
    xf = x.astype(jnp.float32)
    x1, x2 = xf[..., :half], xf[..., half:]
    return jnp.concatenate([x1 * cos - x2 * sin, x1 * sin + x2 * cos], axis=-1).astype(x.dtype)


def dilated_band_attention(q, k, v, dil, n_back):
    B, S, H, Dh = q.shape
    L = S // dil
    nb = -(-L // ATT_BLOCK)
    Lp = nb * ATT_BLOCK

    def streams(t, front):
        t = t.reshape(B, L, dil, H, Dh).transpose(0, 2, 1, 3, 4)
        return jnp.pad(t, ((0, 0), (0, 0), (front, Lp - L), (0, 0), (0, 0)))

    def band(t):
        prev = t[:, :, :Lp].reshape(B, dil, nb, ATT_BLOCK, H, Dh)
        cur = t[:, :, ATT_BLOCK:].reshape(B, dil, nb, ATT_BLOCK, H, Dh)
        return jnp.concatenate([prev, cur], axis=3)

    qb = streams(q, 0).reshape(B, dil, nb, ATT_BLOCK, H, Dh)
    kb = band(streams(k, ATT_BLOCK))
    vb = band(streams(v, ATT_BLOCK))
    s = jnp.einsum('bdnqhc,bdnkhc->bdnhqk', qb, kb).astype(jnp.float32) * (Dh ** -0.5)
    qi = jnp.arange(ATT_BLOCK)[:, None]
    kj = jnp.arange(2 * ATT_BLOCK)[None, :]
    dist = qi + ATT_BLOCK - kj
    key_pos = jnp.arange(nb)[:, None, None] * ATT_BLOCK - ATT_BLOCK + kj[None]
    valid = (dist >= 0)[None] & (dist <= n_back)[None] & (key_pos >= 0)
    s = jnp.where(valid[None, None, :, None], s, NEG_INF)
    mx = jnp.max(s, axis=-1, keepdims=True)
    p = jnp.exp(s - mx)
    den = jnp.sum(p, axis=-1, keepdims=True)
    o = jnp.einsum('bdnhqk,bdnkhc->bdnqhc', (p / den).astype(v.dtype), vb)
    lse = mx[..., 0] + jnp.log(den[..., 0])
    o = o.reshape(B, dil, Lp, H, Dh)[:, :, :L].transpose(0, 2, 1, 3, 4).reshape(B, S, H, Dh)
    lse = lse.transpose(0, 1, 2, 4, 3).reshape(B, dil, Lp, H)[:, :, :L].transpose(0, 2, 1, 3).reshape(B, S, H)
    return o, lse


def dilated_cached_attention(q, k_all, v_all, w_buf, dil, n_back):
    T = q.shape[1]
    idx = w_buf + jnp.arange(T)[:, None] - dil * jnp.arange(n_back + 1)[None, :]
    valid = idx >= 0
    idx = jnp.maximum(idx, 0)
    kg = k_all[:, idx]
    vg = v_all[:, idx]
    s = jnp.einsum('bthc,btjhc->bthj', q, kg).astype(jnp.float32) * (HEAD_DIM ** -0.5)
    s = jnp.where(valid[None, :, None, :], s, NEG_INF)
    mx = jnp.max(s, axis=-1, keepdims=True)
    p = jnp.exp(s - mx)
    den = jnp.sum(p, axis=-1, keepdims=True)
    o = jnp.einsum('bthj,btjhc->bthc', (p / den).astype(v_all.dtype), vg)
    return o, mx[..., 0] + jnp.log(den[..., 0])


def mlstm_chunkwise(q, k, v, i_pre, f_pre, C0, n0, m0, chunk):
    B, S, NH, Dk = q.shape
    nc = S // chunk
    f32 = jnp.float32

    def chunks(t):
        t = t.astype(f32).reshape((B, nc, chunk) + t.shape[2:])
        return jnp.moveaxis(jnp.moveaxis(t, 1, 0), 2, 3)

    qc = chunks(q)
    kc = chunks(k * (Dk ** -0.5))
    vc = chunks(v)
    ic = chunks(i_pre)
    lfc = chunks(jax.nn.log_sigmoid(f_pre.astype(f32)))
    causal = jnp.tril(jnp.ones((chunk, chunk), dtype=bool))

    def step(carry, inp):
        C, n, m = carry
        qq, kk, vv, ii, lf = inp
        b = jnp.cumsum(lf, axis=-1)
        m_t = b + jnp.maximum(m[..., None], lax.cummax(ii - b, axis=2))
        inter = jnp.exp(m[..., None] + b - m_t)
        logd = (ii - b)[..., None, :] + (b - m_t)[..., :, None]
        dmat = jnp.exp(jnp.where(causal, logd, NEG_INF))
        w = jnp.einsum('bhtd,bhsd->bhts', qq, kk) * dmat
        num = inter[..., None] * jnp.einsum('bhtd,bhde->bhte', qq, C) + jnp.einsum('bhts,bhse->bhte', w, vv)
        den = inter * jnp.einsum('bhtd,bhd->bht', qq, n) + jnp.sum(w, axis=-1)
        h = num / jnp.maximum(jnp.abs(den), jnp.exp(-m_t))[..., None]
        m_last = m_t[..., -1]
        decay = jnp.exp(m + b[..., -1] - m_last)
        wk = jnp.exp(ii + b[..., -1:] - b - m_last[..., None])
        C_new = decay[..., None, None] * C + jnp.einsum('bhs,bhsd,bhse->bhde', wk, kk, vv)
        n_new = decay[..., None] * n + jnp.einsum('bhs,bhsd->bhd', wk, kk)
        return (C_new, n_new, m_last), h

    (C, n, m), hs = lax.scan(step, (C0.astype(f32), n0.astype(f32), m0.astype(f32)), (qc, kc, vc, ic, lfc))
    h = jnp.moveaxis(jnp.moveaxis(hs, 3, 2), 0, 1).reshape(B, S, NH, v.shape[-1])
    return h.astype(q.dtype), C, n, m


def route(h, w_router, b_router):
    T = h.shape[0]
    s = jax.nn.sigmoid((h @ w_router).astype(jnp.float32))
    sb = (s + b_router.astype(jnp.float32)).reshape(T, N_EXPERT_GROUPS, EXPERTS_PER_GROUP)
    gscore = jnp.sum(lax.top_k(sb, 2)[0], axis=-1)
    gsel = jnp.argmax(gscore, axis=-1)
    in_group = jnp.take_along_axis(sb, gsel[:, None, None], axis=1)[:, 0]
    _, loc = lax.top_k(in_group, TOP_K)
    eidx = gsel[:, None] * EXPERTS_PER_GROUP + loc
    sel = jnp.take_along_axis(s, eidx, axis=1)
    return eidx, sel / jnp.sum(sel, axis=-1, keepdims=True)


def moe_ffn(h, eidx, wts, w_gate, w_up, w_down):
    T, D = h.shape
    A = T * TOP_K
    e_flat = eidx.reshape(A)
    order = jnp.argsort(e_flat)
    e_sorted = e_flat[order]
    tok_sorted = order // TOP_K
    w_sorted = wts.reshape(A)[order]
    counts = jnp.zeros((N_EXPERTS,), jnp.int32).at[e_flat].add(1)
    padded = (counts + MOE_BLOCK - 1) // MOE_BLOCK * MOE_BLOCK
    pad_end = jnp.cumsum(padded)
    pad_start = pad_end - padded
    start = jnp.cumsum(counts) - counts
    dest = pad_start[e_sorted] + jnp.arange(A) - start[e_sorted]
    n_blocks = -(-A // MOE_BLOCK) + N_EXPERTS
    rows = n_blocks * MOE_BLOCK
    row_tok = jnp.full((rows,), T, jnp.int32).at[dest].set(tok_sorted)
    xs = jnp.concatenate([h, jnp.zeros((1, D), h.dtype)], axis=0)[row_tok].reshape(n_blocks, MOE_BLOCK, D)
    blk_exp = jnp.minimum(jnp.searchsorted(pad_end, jnp.arange(n_blocks) * MOE_BLOCK, side='right'), N_EXPERTS - 1)

    def expert_block(args):
        xb, e = args
        return (jax.nn.silu(xb @ w_gate[e]) * (xb @ w_up[e])) @ w_down[e]

    ys = lax.map(expert_block, (xs, blk_exp)).reshape(rows, D)
    return jnp.zeros((T, D), h.dtype).at[tok_sorted].add(ys[dest] * w_sorted[:, None].astype(h.dtype))


def run_group(x, c, pos, win_caches, st_c, st_n, st_m, weights, sample):
    (w_ada, b_ada, g_mix, g_ffn, w_in, b_igate, b_fgate, g_mlstm_out, w_att_out, w_mlstm_out,
     w_mix_out, w_router, b_router, w_exp_gate, w_exp_up, w_exp_down, g_final) = weights
    B, S, D = x.shape
    cs = jax.nn.silu(c)
    new_win = [[] for _ in ATT_GROUPS]
    new_c, new_n, new_m = [], [], []
    for l in range(DEPTH):
        mod = (cs @ w_ada[l] + b_ada[l])[:, None, :]
        sh1, sc1, gt1, sh2, sc2, gt2 = jnp.split(mod, 6, axis=-1)
        h = rmsnorm(x, g_mix[l]) * (1 + sc1) + sh1
        qa, ka, va, qm, km, vm, ig, fg, og, ga, gb = jnp.split(h @ w_in[l], IN_OFFSETS, axis=-1)
        qa = rotary(qa.reshape(B, S, N_ATT_HEADS, HEAD_DIM), pos)
        ka = rotary(ka.reshape(B, S, N_ATT_HEADS, HEAD_DIM), pos)
        va = va.reshape(B, S, N_ATT_HEADS, HEAD_DIM)
        outs, lses = [], []
        for g, (win, dil) in enumerate(ATT_GROUPS):
            sl = slice(g * ATT_HEADS_PER_GROUP, (g + 1) * ATT_HEADS_PER_GROUP)
            qg, kg, vg = qa[:, :, sl], ka[:, :, sl], va[:, :, sl]
            if sample:
                buf = win_caches[g][l]
                w_buf = buf.shape[1]
                o, lse = dilated_cached_attention(qg, jnp.concatenate([buf[:, :, 0], kg], axis=1),
                                                  jnp.concatenate([buf[:, :, 1], vg], axis=1), w_buf, dil, win // dil)
                new_win[g].append(jnp.stack([kg, vg], axis=2))
            else:
                o, lse = dilated_band_attention(qg, kg, vg, dil, win // dil)
                keep = min(win, S)
                new_win[g].append(jnp.stack([kg[:, S - keep:], vg[:, S - keep:]], axis=2))
            outs.append(o)
            lses.append(lse)
        alpha = jax.nn.softmax(jnp.stack(lses, axis=0), axis=0)
        att = jnp.einsum('gbsh,gbshc->bshc', alpha.astype(x.dtype), jnp.stack(outs, axis=0)).reshape(B, S, ATT_OUT_WIDTH)
        if sample:
            C0, n0, m0, chunk = st_c[l], st_n[l], st_m[l], S
        else:
            C0 = jnp.zeros((B, M_HEADS, M_HEAD_DIM, M_HEAD_DIM), jnp.float32)
            n0 = jnp.zeros((B, M_HEADS, M_HEAD_DIM), jnp.float32)
            m0 = jnp.zeros((B, M_HEADS), jnp.float32)
            chunk = min(M_CHUNK, S)
        hm, C1, n1, m1 = mlstm_chunkwise(qm.reshape(B, S, M_HEADS, M_HEAD_DIM), km.reshape(B, S, M_HEADS, M_HEAD_DIM),
                                         vm.reshape(B, S, M_HEADS, M_HEAD_DIM), ig + b_igate[l], fg + b_fgate[l],
                                         C0, n0, m0, chunk)
        new_c.append(C1.astype(x.dtype))
        new_n.append(n1.astype(x.dtype))
        new_m.append(m1.astype(x.dtype))
        hm = rmsnorm(hm, g_mlstm_out[l].reshape(M_HEADS, M_HEAD_DIM)) * jax.nn.sigmoid(og).reshape(B, S, M_HEADS, M_HEAD_DIM)
        hm = hm.reshape(B, S, M_WIDTH)
        merged = jax.nn.sigmoid(ga) * (att @ w_att_out[l]) + jax.nn.sigmoid(gb) * (hm @ w_mlstm_out[l])
        x = x + gt1 * (merged @ w_mix_out[l])
        hf = (rmsnorm(x, g_ffn[l]) * (1 + sc2) + sh2).reshape(B * S, D)
        eidx, wts = route(hf, w_router, b_router)
        x = x + gt2 * moe_ffn(hf, eidx, wts, w_exp_gate[l], w_exp_up[l], w_exp_down[l]).reshape(B, S, D)
    y = rmsnorm(x, g_final)
    wins = [jnp.stack(new_win[g], axis=0) for g in range(len(ATT_GROUPS))]
    return y, wins, jnp.stack(new_c, axis=0), jnp.stack(new_n, axis=0), jnp.stack(new_m, axis=0)


def setup_inputs(seed: int = 0) -> dict:
    key = jax.random.key(seed)
    ks = jax.random.split(key, 32)

    def nrm(k, shape, scale):
        return jax.random.normal(k, shape, jnp.float32) * scale

    D = D_MODEL
    win_shape = lambda w: (DEPTH, DEC_BATCH, min(w, PAST_LEN), 2, ATT_HEADS_PER_GROUP, HEAD_DIM)
    return {
        "x_prompt": nrm(ks[0], (BATCH, SEQ, D), 1.0),
        "x_sample": nrm(ks[1], (DEC_BATCH, DEC_SEQ, D), 1.0),
        "cache_win_w128": nrm(ks[2], win_shape(ATT_GROUPS[0][0]), 1.0),
        "cache_win_w512": nrm(ks[3], win_shape(ATT_GROUPS[1][0]), 1.0),
        "cache_win_w2048": nrm(ks[4], win_shape(ATT_GROUPS[2][0]), 1.0),
        "state_mlstm_c": nrm(ks[5], (DEPTH, DEC_BATCH, M_HEADS, M_HEAD_DIM, M_HEAD_DIM), 0.05),
        "state_mlstm_n": nrm(ks[6], (DEPTH, DEC_BATCH, M_HEADS, M_HEAD_DIM), 0.05),
        "state_mlstm_m": nrm(ks[7], (DEPTH, DEC_BATCH, M_HEADS), 1.0),
        "c_prompt": nrm(ks[8], (BATCH, D), 1.0),
        "c_sample": nrm(ks[9], (DEC_BATCH, D), 1.0),
        "w_ada": nrm(ks[10], (DEPTH, D, 6 * D), 0.5 * D ** -0.5),
        "b_ada": nrm(ks[11], (DEPTH, 6 * D), 0.02),
        "g_mix": 1.0 + nrm(ks[12], (DEPTH, D), 0.05),
        "g_ffn": 1.0 + nrm(ks[13], (DEPTH, D), 0.05),
        "w_in": nrm(ks[14], (DEPTH, D, D_IN), D ** -0.5),
        "b_igate": nrm(ks[15], (DEPTH, M_HEADS), 0.1),
        "b_fgate": jnp.linspace(3.0, 6.0, M_HEADS, dtype=jnp.float32)[None, :] + nrm(ks[16], (DEPTH, M_HEADS), 0.1),
        "g_mlstm_out": 1.0 + nrm(ks[17], (DEPTH, M_WIDTH), 0.05),
        "w_att_out": nrm(ks[18], (DEPTH, ATT_OUT_WIDTH, D), ATT_OUT_WIDTH ** -0.5),
        "w_mlstm_out": nrm(ks[19], (DEPTH, M_WIDTH, D), M_WIDTH ** -0.5),
        "w_mix_out": nrm(ks[20], (DEPTH, D, D), D ** -0.5),
        "w_router": nrm(ks[21], (D, N_EXPERTS), D ** -0.5),
        "b_router": nrm(ks[22], (N_EXPERTS,), 0.01),
        "w_exp_gate": nrm(ks[23], (DEPTH, N_EXPERTS, D, D_EXPERT), D ** -0.5),
        "w_exp_up": nrm(ks[24], (DEPTH, N_EXPERTS, D, D_EXPERT), D ** -0.5),
        "w_exp_down": nrm(ks[25], (DEPTH, N_EXPERTS, D_EXPERT, D), D_EXPERT ** -0.5),
        "g_final": 1.0 + nrm(ks[26], (D,), 0.05),
    }


def reference(x_prompt, x_sample, cache_win_w128, cache_win_w512, cache_win_w2048, state_mlstm_c, state_mlstm_n,
              state_mlstm_m, c_prompt, c_sample, w_ada, b_ada, g_mix, g_ffn, w_in, b_igate, b_fgate, g_mlstm_out,
              w_att_out, w_mlstm_out, w_mix_out, w_router, b_router, w_exp_gate, w_exp_up, w_exp_down, g_final):
    weights = (w_ada, b_ada, g_mix, g_ffn, w_in, b_igate, b_fgate, g_mlstm_out, w_att_out, w_mlstm_out,
               w_mix_out, w_router, b_router, w_exp_gate, w_exp_up, w_exp_down, g_final)
    pos_p = jnp.arange(x_prompt.shape[1], dtype=jnp.int32)
    pos_s = PAST_LEN + jnp.arange(x_sample.shape[1], dtype=jnp.int32)
    y_prompt, win_p, c_p, n_p, m_p = run_group(x_prompt, c_prompt, pos_p, None, None, None, None, weights, False)
    y_sample, win_s, c_s, n_s, m_s = run_group(x_sample, c_sample, pos_s,
                                               (cache_win_w128, cache_win_w512, cache_win_w2048),
                                               state_mlstm_c, state_mlstm_n, state_mlstm_m, weights, True)
    return (y_prompt, y_sample, win_p[0], win_p[1], win_p[2], c_p, n_p, m_p,
            win_s[0], win_s[1], win_s[2], c_s, n_s, m_s)
```

```python
import functools

import numpy as np
import jax
import jax.numpy as jnp
from jax import lax
from jax.experimental import pallas as pl
from jax.experimental.pallas import tpu as pltpu

F32 = jnp.float32
BF16 = jnp.bfloat16
I32 = jnp.int32

D_MODEL = 2048
DEPTH = 4
PAST_LEN = 16384
HEAD_DIM = 128
ATT_GROUPS = ((128, 1), (512, 4), (2048, 16))
ATT_HPG = 4
ATT_WIDTH = len(ATT_GROUPS) * ATT_HPG * HEAD_DIM
ATT_OUT = ATT_HPG * HEAD_DIM
ATT_BLOCK = 128
ROPE_THETA = 10000.0
M_HEADS = 4
M_WIDTH = D_MODEL // 2
M_HD = M_WIDTH // M_HEADS
N_EXPERTS = 16
N_EGROUPS = 4
EPG = N_EXPERTS // N_EGROUPS
TOP_K = 2
D_EXPERT = D_MODEL // 4
EPS = 1e-6
NEG_INF = -1e30
D_IN = 3 * ATT_WIDTH + 3 * M_WIDTH + 2 * M_HEADS + M_WIDTH + 2 * D_MODEL
GATE_OFF = 3 * ATT_WIDTH + 3 * M_WIDTH
TAIL_OFF = GATE_OFF + 2 * M_HEADS

V7X_VMEM_BYTES = 64 * 1024 * 1024
LANES = 128
SUBLANES = 8
VMEM_LIMIT = 56 * 1024 * 1024

MLSTM_CHUNK = 256
MOE_BLOCK_PROMPT = 512
MOE_BLOCK_SAMPLE = 8


def _pcall(body, grid, in_specs, out_specs, out_shape, scratch=(), sem=None, name=None):
    return pl.pallas_call(
        body,
        grid_spec=pltpu.PrefetchScalarGridSpec(
            num_scalar_prefetch=1, grid=grid, in_specs=in_specs, out_specs=out_specs,
            scratch_shapes=list(scratch)),
        out_shape=out_shape,
        compiler_params=pltpu.CompilerParams(dimension_semantics=sem, vmem_limit_bytes=VMEM_LIMIT),
        name=name)


def _split(x):
    hi = x.astype(BF16)
    lo = (x - hi.astype(F32)).astype(BF16)
    return hi, lo


def _dot(a, b):
    return jnp.dot(a, b, preferred_element_type=F32)


def _dot_nt(a, b):
    return lax.dot_general(a, b, (((1,), (1,)), ((), ())), preferred_element_type=F32)


def _dot_tn(a, b):
    return lax.dot_general(a, b, (((0,), (0,)), ((), ())), preferred_element_type=F32)


def _dot3(a_hi, a_lo, w_hi, w_lo):
    return _dot(a_hi, w_hi) + (_dot(a_hi, w_lo) + _dot(a_lo, w_hi))


def _sigmoid(x):
    return 1.0 / (1.0 + jnp.exp(-x))


def _log_sigmoid(x):
    return jnp.minimum(x, 0.0) - jnp.log1p(jnp.exp(-jnp.abs(x)))


def _ada_body(meta, c_ref, w_ref, b_ref, o_ref):
    c = c_ref[...]
    a_hi, a_lo = _split(c * _sigmoid(c))
    w_hi, w_lo = _split(w_ref[...])
    o_ref[...] = _dot3(a_hi, a_lo, w_hi, w_lo) + b_ref[...]


def ada_modulation(c_all, w_ada, b_ada):
    rows = c_all.shape[0]
    n_out = w_ada.shape[2]
    tn = 1024
    meta = jnp.zeros((1,), I32)
    return _pcall(
        _ada_body, (DEPTH, n_out // tn),
        [pl.BlockSpec((rows, D_MODEL), lambda l, j, m: (0, 0)),
         pl.BlockSpec((None, D_MODEL, tn), lambda l, j, m: (l, 0, j)),
         pl.BlockSpec((None, 1, tn), lambda l, j, m: (l, 0, j))],
        pl.BlockSpec((None, rows, tn), lambda l, j, m: (l, 0, j)),
        jax.ShapeDtypeStruct((DEPTH, rows, n_out), F32),
        sem=("parallel", "parallel"), name="ada_mod")(meta, c_all, w_ada, b_ada.reshape(DEPTH, 1, n_out))


def _rms(x, g):
    return x * lax.rsqrt(jnp.mean(x * x, axis=-1, keepdims=True) + EPS) * g


def _norm_mod_body(meta, x_ref, g_ref, sc_ref, sh_ref, o_ref):
    y = _rms(x_ref[...], g_ref[...])
    o_ref[...] = (y * (1.0 + sc_ref[...]) + sh_ref[...]).astype(o_ref.dtype)


def _norm_plain_body(meta, x_ref, g_ref, o_ref):
    o_ref[...] = _rms(x_ref[...], g_ref[...]).astype(o_ref.dtype)


def _row_tile(S):
    return min(S, 512)


def norm_mod(meta, x3, g_all, sc, sh, out_dtype):
    B, S, D = x3.shape
    ts = _row_tile(S)
    return _pcall(
        _norm_mod_body, (B, S // ts),
        [pl.BlockSpec((None, ts, D), lambda b, i, m: (b, i, 0)),
         pl.BlockSpec((None, 1, D), lambda b, i, m: (m[0], 0, 0)),
         pl.BlockSpec((None, 1, D), lambda b, i, m: (b, 0, 0)),
         pl.BlockSpec((None, 1, D), lambda b, i, m: (b, 0, 0))],
        pl.BlockSpec((None, ts, D), lambda b, i, m: (b, i, 0)),
        jax.ShapeDtypeStruct((B, S, D), out_dtype),
        sem=("parallel", "parallel"), name="norm_mod")(meta, x3, g_all, sc, sh)


def _norm_streams_body(meta, x_ref, g_ref, sc_ref, sh_ref, o0_ref, o1_ref, o2_ref, y_s, *, ts):
    y = _rms(x_ref[...], g_ref[...]) * (1.0 + sc_ref[...]) + sh_ref[...]
    o0_ref[...] = y.astype(o0_ref.dtype)
    n_slabs = y.shape[1] // LANES
    for c in range(n_slabs):
        y_s[c] = y[:, c * LANES:(c + 1) * LANES]
    for o_ref, (_, dil) in ((o1_ref, ATT_GROUPS[1]), (o2_ref, ATT_GROUPS[2])):
        for r in range(dil):
            for c in range(n_slabs):
                o_ref[r, :, c * LANES:(c + 1) * LANES] = (
                    y_s[c, pl.ds(r, ts // dil, stride=dil), :].astype(o_ref.dtype))


def norm_mod_streams(meta, x3, g_all, sc, sh):
    B, S, D = x3.shape
    ts = _row_tile(S)
    d1, d2 = ATT_GROUPS[1][1], ATT_GROUPS[2][1]
    outs = _pcall(
        functools.partial(_norm_streams_body, ts=ts), (B, S // ts),
        [pl.BlockSpec((None, ts, D), lambda b, i, m: (b, i, 0)),
         pl.BlockSpec((None, 1, D), lambda b, i, m: (m[0], 0, 0)),
         pl.BlockSpec((None, 1, D), lambda b, i, m: (b, 0, 0)),
         pl.BlockSpec((None, 1, D), lambda b, i, m: (b, 0, 0))],
        [pl.BlockSpec((None, ts, D), lambda b, i, m: (b, i, 0)),
         pl.BlockSpec((None, d1, ts // d1, D), lambda b, i, m: (b, 0, i, 0)),
         pl.BlockSpec((None, d2, ts // d2, D), lambda b, i, m: (b, 0, i, 0))],
        [jax.ShapeDtypeStruct((B, S, D), BF16),
         jax.ShapeDtypeStruct((B, d1, S // d1, D), BF16),
         jax.ShapeDtypeStruct((B, d2, S // d2, D), BF16)],
        scratch=[pltpu.VMEM((D // LANES, ts, LANES), F32)],
        sem=("parallel", "parallel"), name="norm_mod_streams")(meta, x3, g_all, sc, sh)
    return [o.reshape(B * S, D) for o in outs]


def norm_final(x3, g):
    B, S, D = x3.shape
    ts = _row_tile(S)
    meta = jnp.zeros((1,), I32)
    return _pcall(
        _norm_plain_body, (B, S // ts),
        [pl.BlockSpec((None, ts, D), lambda b, i, m: (b, i, 0)),
         pl.BlockSpec((1, D), lambda b, i, m: (0, 0))],
        pl.BlockSpec((None, ts, D), lambda b, i, m: (b, i, 0)),
        jax.ShapeDtypeStruct((B, S, D), F32),
        sem=("parallel", "parallel"), name="norm_final")(meta, x3, g.reshape(1, D))


def _matmul_tile(a_ref, w_ref, precise):
    w = w_ref[...]
    if precise:
        a_hi, a_lo = _split(a_ref[...])
        w_hi, w_lo = _split(w)
        return _dot3(a_hi, a_lo, w_hi, w_lo)
    return _dot(a_ref[...], w.astype(BF16))


def _linear_body(meta, a_ref, w_ref, o_ref, *, precise):
    o_ref[...] = _matmul_tile(a_ref, w_ref, precise).astype(o_ref.dtype)


def _linear_rot_body(meta, a_ref, w_ref, cos_ref, sin_ref, o_ref, *, precise, n_rot, tn):
    acc = _matmul_tile(a_ref, w_ref, precise)
    j = pl.program_id(1)

    @pl.when(j < n_rot)
    def _():
        c = cos_ref[...]
        s = sin_ref[...]
        for h in range(tn // HEAD_DIM):
            hs = slice(h * HEAD_DIM, (h + 1) * HEAD_DIM)
            x = acc[:, hs]
            o_ref[:, hs] = (x * c + pltpu.roll(x, HEAD_DIM // 2, 1) * s).astype(o_ref.dtype)

    @pl.when(j >= n_rot)
    def _():
        o_ref[...] = acc.astype(o_ref.dtype)


def _linear_res_body(meta, a_ref, w_ref, x_ref, gt_ref, o_ref, *, precise):
    o_ref[...] = x_ref[...] + gt_ref[...] * _matmul_tile(a_ref, w_ref, precise)


def _m_tile(T):
    return min(T, 1024)


def linear(meta, a, w, col_off, n_out, out_dtype, precise, tn=512, rot=None, col_step=1):
    T, K = a.shape
    tm = _m_tile(T)
    off = col_off // tn
    in_specs = [pl.BlockSpec((tm, K), lambda i, j, m: (i, 0)),
                pl.BlockSpec((None, K, tn), lambda i, j, m: (m[0], 0, off + j * col_step))]
    args = [a, w]
    if rot is None:
        body = functools.partial(_linear_body, precise=precise)
    else:
        cos_t, sin_t, n_rot = rot
        nper = cos_t.shape[0] // tm
        in_specs += [pl.BlockSpec((tm, HEAD_DIM), lambda i, j, m: (i % nper, 0)),
                     pl.BlockSpec((tm, HEAD_DIM), lambda i, j, m: (i % nper, 0))]
        args += [cos_t, sin_t]
        body = functools.partial(_linear_rot_body, precise=precise, n_rot=n_rot, tn=tn)
    return _pcall(
        body, (T // tm, n_out // tn), in_specs,
        pl.BlockSpec((tm, tn), lambda i, j, m: (i, j)),
        jax.ShapeDtypeStruct((T, n_out), out_dtype),
        sem=("parallel", "arbitrary"), name="linear")(meta, *args)


def linear_residual(meta, a, w, x2, gt, rows_per_batch, precise, tn=512):
    T, K = a.shape
    D = x2.shape[1]
    tm = _m_tile(T)
    if rows_per_batch == 1:
        gt_arr = gt.reshape(T, D)
        gt_spec = pl.BlockSpec((tm, tn), lambda i, j, m: (i, j))
    else:
        per = rows_per_batch // tm
        gt_arr = gt
        gt_spec = pl.BlockSpec((None, 1, tn), lambda i, j, m: (i // per, 0, j))
    return _pcall(
        functools.partial(_linear_res_body, precise=precise), (T // tm, D // tn),
        [pl.BlockSpec((tm, K), lambda i, j, m: (i, 0)),
         pl.BlockSpec((None, K, tn), lambda i, j, m: (m[0], 0, j)),
         pl.BlockSpec((tm, tn), lambda i, j, m: (i, j)),
         gt_spec],
        pl.BlockSpec((tm, tn), lambda i, j, m: (i, j)),
        jax.ShapeDtypeStruct((T, D), F32),
        sem=("parallel", "arbitrary"), name="linear_residual")(meta, a, w, x2, gt_arr)


def _attn_body(meta, q_ref, k_ref, v_ref, o_ref, lse_ref, *, nb):
    scale = HEAD_DIM ** -0.5
    qi = lax.broadcasted_iota(I32, (ATT_BLOCK, ATT_BLOCK), 0)
    kj = lax.broadcasted_iota(I32, (ATT_BLOCK, ATT_BLOCK), 1)
    m_cur = kj <= qi
    m_prev = qi <= kj

    def blk(n, carry):
        qs = pl.multiple_of(n * ATT_BLOCK, ATT_BLOCK)
        ps = pl.multiple_of(jnp.maximum(n - 1, 0) * ATT_BLOCK, ATT_BLOCK)
        prev_cap = jnp.where(n > 0, -NEG_INF, NEG_INF)
        for h in range(ATT_HPG):
            hs = slice(h * HEAD_DIM, (h + 1) * HEAD_DIM)
            q = q_ref[pl.ds(qs, ATT_BLOCK), hs].astype(BF16)
            kc = k_ref[pl.ds(qs, ATT_BLOCK), hs].astype(BF16)
            kp = k_ref[pl.ds(ps, ATT_BLOCK), hs].astype(BF16)
            vc = v_ref[pl.ds(qs, ATT_BLOCK), hs].astype(BF16)
            vp = v_ref[pl.ds(ps, ATT_BLOCK), hs].astype(BF16)
            sc = jnp.where(m_cur, _dot_nt(q, kc) * scale, NEG_INF)
            sp = jnp.minimum(jnp.where(m_prev, _dot_nt(q, kp) * scale, NEG_INF), prev_cap)
            mx = jnp.maximum(jnp.max(sc, axis=1, keepdims=True), jnp.max(sp, axis=1, keepdims=True))
            pc = jnp.exp(sc - mx)
            pp = jnp.exp(sp - mx)
            den = jnp.sum(pc, axis=1, keepdims=True) + jnp.sum(pp, axis=1, keepdims=True)
            o = (_dot(pc.astype(BF16), vc) + _dot(pp.astype(BF16), vp)) / den
            o_ref[pl.ds(qs, ATT_BLOCK), hs] = o.astype(o_ref.dtype)
            lse_ref[pl.ds(qs, ATT_BLOCK), hs] = jnp.broadcast_to(mx + jnp.log(den), (ATT_BLOCK, HEAD_DIM))
        return carry

    lax.fori_loop(0, nb, blk, 0)


def band_attention(qkv_g, B, S, g):
    _, dil = ATT_GROUPS[g]
    L = S // dil
    assert L % ATT_BLOCK == 0
    qv = qkv_g.reshape(B, dil, L, 3 * ATT_OUT)
    meta = jnp.zeros((1,), I32)
    o, lse = _pcall(
        functools.partial(_attn_body, nb=L // ATT_BLOCK), (B, dil),
        [pl.BlockSpec((None, None, L, ATT_OUT), lambda b, r, m: (b, r, 0, 0)),
         pl.BlockSpec((None, None, L, ATT_OUT), lambda b, r, m: (b, r, 0, 1)),
         pl.BlockSpec((None, None, L, ATT_OUT), lambda b, r, m: (b, r, 0, 2))],
        [pl.BlockSpec((None, L, ATT_OUT), lambda b, r, m: (b, 0, r)),
         pl.BlockSpec((None, L, ATT_OUT), lambda b, r, m: (b, 0, r))],
        [jax.ShapeDtypeStruct((B, L, dil * ATT_OUT), BF16),
         jax.ShapeDtypeStruct((B, L, dil * ATT_OUT), F32)],
        sem=("parallel", "parallel"), name="band_attention")(meta, qv, qv, qv)
    return o.reshape(B * S, ATT_OUT), lse.reshape(B * S, ATT_OUT)


def _attn_merge_body(meta, o0, o1, o2, l0, l1, l2, att_ref):
    a0, a1, a2 = l0[...], l1[...], l2[...]
    mx = jnp.maximum(jnp.maximum(a0, a1), a2)
    e0, e1, e2 = jnp.exp(a0 - mx), jnp.exp(a1 - mx), jnp.exp(a2 - mx)
    num = e0 * o0[...].astype(F32) + e1 * o1[...].astype(F32) + e2 * o2[...].astype(F32)
    att_ref[...] = (num / (e0 + e1 + e2)).astype(att_ref.dtype)


def attn_merge(outs, lses):
    T = outs[0].shape[0]
    tm = _m_tile(T)
    spec = pl.BlockSpec((tm, ATT_OUT), lambda i, m: (i, 0))
    meta = jnp.zeros((1,), I32)
    return _pcall(_attn_merge_body, (T // tm,), [spec] * 6, spec,
                  jax.ShapeDtypeStruct((T, ATT_OUT), BF16),
                  sem=("parallel",), name="attn_merge")(meta, *outs, *lses)


def _attn_sample_body(meta, q_ref, c0_ref, c1_ref, c2_ref, att_ref):
    scale = HEAD_DIM ** -0.5
    row = q_ref[...]
    caches = (c0_ref, c1_ref, c2_ref)
    for h in range(ATT_HPG):
        outs, lses = [], []
        for g in range(len(ATT_GROUPS)):
            col = (g * ATT_HPG + h) * HEAD_DIM
            q = row[:, col:col + HEAD_DIM]
            k_new = row[:, ATT_WIDTH + col:ATT_WIDTH + col + HEAD_DIM]
            v_new = row[:, 2 * ATT_WIDTH + col:2 * ATT_WIDTH + col + HEAD_DIM]
            kc = caches[g][:, 0, h, :]
            vc = caches[g][:, 1, h, :]
            s = jnp.sum(kc * q, axis=1, keepdims=True) * scale
            s0 = jnp.sum(k_new * q, axis=1, keepdims=True) * scale
            mx = jnp.maximum(jnp.max(s, axis=0, keepdims=True), s0)
            p = jnp.exp(s - mx)
            p0 = jnp.exp(s0 - mx)
            den = jnp.sum(p, axis=0, keepdims=True) + p0
            outs.append((jnp.sum(vc * p, axis=0, keepdims=True) + p0 * v_new) / den)
            lses.append(mx + jnp.log(den))
        mxl = jnp.maximum(jnp.maximum(lses[0], lses[1]), lses[2])
        es = [jnp.exp(l - mxl) for l in lses]
        num = es[0] * outs[0] + es[1] * outs[1] + es[2] * outs[2]
        att_ref[:, h * HEAD_DIM:(h + 1) * HEAD_DIM] = num / (es[0] + es[1] + es[2])


def sample_attention(meta, qkv, caches):
    Bd = qkv.shape[0]
    wcols = 3 * ATT_WIDTH
    in_specs = [pl.BlockSpec((None, 1, wcols), lambda b, m: (b, 0, 0))]
    args = [qkv.reshape(Bd, 1, wcols)]
    for g, (win, dil) in enumerate(ATT_GROUPS):
        c = caches[g]
        assert c.shape[2] == win and win == ATT_BLOCK * dil
        args.append(c.reshape(DEPTH, Bd, ATT_BLOCK, dil, 2, ATT_HPG, HEAD_DIM))
        in_specs.append(pl.BlockSpec((None, None, ATT_BLOCK, None, 2, ATT_HPG, HEAD_DIM),
                                     lambda b, m: (m[0], b, 0, 0, 0, 0, 0)))
    att = _pcall(
        _attn_sample_body, (Bd,), in_specs,
        pl.BlockSpec((None, 1, ATT_OUT), lambda b, m: (b, 0, 0)),
        jax.ShapeDtypeStruct((Bd, 1, ATT_OUT), F32),
        sem=("parallel",), name="sample_attention")(meta, *args)
    return att.reshape(Bd, ATT_OUT)


def _mlstm_body(meta, q_ref, k_ref, v_ref, gc_ref, gr_ref, bc_ref, br_ref, og_ref, g_ref,
                hm_ref, caug_ref, m_ref, c_s, m_s, *, Lc, nc):
    h = pl.program_id(1)
    c = pl.program_id(2)

    @pl.when(c == 0)
    def _():
        c_s[...] = jnp.zeros_like(c_s)
        m_s[...] = jnp.zeros_like(m_s)

    gcol = gc_ref[...] + bc_ref[...]
    lane = lax.broadcasted_iota(I32, gcol.shape, 1)
    i_col = jnp.sum(jnp.where(lane == h, gcol, 0.0), axis=1, keepdims=True)
    f_col = jnp.sum(jnp.where(lane == h + M_HEADS, gcol, 0.0), axis=1, keepdims=True)
    grow = gr_ref[...] + br_ref[...]
    sub = lax.broadcasted_iota(I32, grow.shape, 0)
    i_row = jnp.sum(jnp.where(sub == h, grow, 0.0), axis=0, keepdims=True)
    f_row = jnp.sum(jnp.where(sub == h + M_HEADS, grow, 0.0), axis=0, keepdims=True)
    lf_col = _log_sigmoid(f_col)
    lf_row = _log_sigmoid(f_row)

    t_i = lax.broadcasted_iota(I32, (Lc, Lc), 0)
    s_i = lax.broadcasted_iota(I32, (Lc, Lc), 1)
    causal = s_i <= t_i
    b_col = jnp.sum(jnp.where(causal, lf_row, 0.0), axis=1, keepdims=True)
    b_row = jnp.sum(jnp.where(t_i <= s_i, lf_col, 0.0), axis=0, keepdims=True)
    a_row = i_row - b_row
    cmax_col = jnp.max(jnp.where(causal, a_row, -jnp.inf), axis=1, keepdims=True)
    m_prev = m_s[0:1, 0:1]
    m_t = b_col + jnp.maximum(m_prev, cmax_col)
    inter = jnp.exp(m_prev + b_col - m_t)
    dmat = jnp.exp(jnp.where(causal, a_row + (b_col - m_t), NEG_INF))

    q = q_ref[...]
    k = k_ref[...]
    v = v_ref[...]
    kscale = M_HD ** -0.5
    one_col = jnp.where(lax.broadcasted_iota(I32, (Lc, LANES), 1) == 0, 1.0, 0.0).astype(BF16)
    v_aug = jnp.concatenate([v, one_col], axis=1)
    w = _dot_nt(q, k) * kscale * dmat
    caug = c_s[...]
    num = inter * _dot(q, caug.astype(BF16)) + _dot(w.astype(BF16), v_aug)
    den = num[:, M_HD:M_HD + 1]
    hh = num[:, :M_HD] / jnp.maximum(jnp.abs(den), jnp.exp(-m_t))

    b_last = jnp.sum(lf_row, axis=1, keepdims=True)
    m_last = b_last + jnp.maximum(m_prev, jnp.max(a_row, axis=1, keepdims=True))
    decay = jnp.exp(m_prev + b_last - m_last)
    wk_col = jnp.exp(i_col + b_last - b_col - m_last)
    kw = (k.astype(F32) * (wk_col * kscale)).astype(BF16)
    c_new = decay * caug + _dot_tn(kw, v_aug)
    c_s[...] = c_new
    m_s[...] = jnp.broadcast_to(m_last, m_s.shape)

    y = _rms(hh, g_ref[...]) * _sigmoid(og_ref[...].astype(F32))
    hm_ref[...] = y.astype(hm_ref.dtype)

    @pl.when(c == nc - 1)
    def _():
        caug_ref[...] = c_new
        m_ref[...] = jnp.broadcast_to(m_last, m_ref.shape)


def mlstm_prompt(meta, qkvm, gates, b_i, b_f, tail, g_out, B, S):
    Lc = min(MLSTM_CHUNK, S)
    nc = S // Lc
    caw = M_HD + LANES
    q3 = qkvm.reshape(B, S, 3 * M_WIDTH)
    g3 = gates.reshape(B, S, LANES)
    gr = jnp.swapaxes(g3[:, :, :SUBLANES], 1, 2)
    bias = jnp.concatenate([b_i, b_f], axis=1)
    bc = jnp.pad(bias, ((0, 0), (0, LANES - 2 * M_HEADS))).reshape(DEPTH, 1, LANES)
    br = bias.reshape(DEPTH, 2 * M_HEADS, 1)
    t3 = tail.reshape(B, S, tail.shape[1])
    hm, caug, mfin = _pcall(
        functools.partial(_mlstm_body, Lc=Lc, nc=nc), (B, M_HEADS, nc),
        [pl.BlockSpec((None, Lc, M_HD), lambda b, h, c, m: (b, c, h)),
         pl.BlockSpec((None, Lc, M_HD), lambda b, h, c, m: (b, c, M_HEADS + h)),
         pl.BlockSpec((None, Lc, M_HD), lambda b, h, c, m: (b, c, 2 * M_HEADS + h)),
         pl.BlockSpec((None, Lc, LANES), lambda b, h, c, m: (b, c, 0)),
         pl.BlockSpec((None, SUBLANES, Lc), lambda b, h, c, m: (b, 0, c)),
         pl.BlockSpec((None, 1, LANES), lambda b, h, c, m: (m[0], 0, 0)),
         pl.BlockSpec((None, 2 * M_HEADS, 1), lambda b, h, c, m: (m[0], 0, 0)),
         pl.BlockSpec((None, Lc, M_HD), lambda b, h, c, m: (b, c, h)),
         pl.BlockSpec((None, 1, M_HD), lambda b, h, c, m: (m[0], 0, h))],
        [pl.BlockSpec((None, Lc, M_HD), lambda b, h, c, m: (b, c, h)),
         pl.BlockSpec((None, None, M_HD, caw), lambda b, h, c, m: (b, h, 0, 0)),
         pl.BlockSpec((None, None, SUBLANES, LANES), lambda b, h, c, m: (b, h, 0, 0))],
        [jax.ShapeDtypeStruct((B, S, M_WIDTH), BF16),
         jax.ShapeDtypeStruct((B, M_HEADS, M_HD, caw), F32),
         jax.ShapeDtypeStruct((B, M_HEADS, SUBLANES, LANES), F32)],
        scratch=[pltpu.VMEM((M_HD, caw), F32), pltpu.VMEM((SUBLANES, LANES), F32)],
        sem=("parallel", "parallel", "arbitrary"), name="mlstm_prompt")(
            meta, q3, q3, q3, g3, gr, bc, br, t3, g_out.reshape(DEPTH, 1, M_WIDTH))
    return (hm.reshape(B * S, M_WIDTH), caug[..., :M_HD], caug[..., M_HD], mfin[:, :, 0, 0])


def _mlstm_sample_body(meta, x_ref, g_ref, bc_ref, og_ref, gout_ref, c0_ref, n0_ref, m0_ref,
                       hm_ref, c1_ref, n1_ref, m1_ref):
    row = x_ref[...]
    gates = g_ref[...] + bc_ref[...]
    eye = lax.broadcasted_iota(I32, (M_HD, M_HD), 0) == lax.broadcasted_iota(I32, (M_HD, M_HD), 1)
    kscale = M_HD ** -0.5
    for h in range(M_HEADS):
        hs = slice(h * M_HD, (h + 1) * M_HD)
        q = row[:, h * M_HD:(h + 1) * M_HD]
        k = row[:, M_WIDTH + h * M_HD:M_WIDTH + (h + 1) * M_HD] * kscale
        v = row[:, 2 * M_WIDTH + h * M_HD:2 * M_WIDTH + (h + 1) * M_HD]
        q_col = jnp.sum(jnp.where(eye, q, 0.0), axis=1, keepdims=True)
        k_col = jnp.sum(jnp.where(eye, k, 0.0), axis=1, keepdims=True)
        ii = gates[:, h:h + 1]
        lf = _log_sigmoid(gates[:, M_HEADS + h:M_HEADS + h + 1])
        C = c0_ref[h]
        n = n0_ref[h:h + 1, :]
        m = m0_ref[:, h:h + 1]
        a = ii - lf
        m_t = lf + jnp.maximum(m, a)
        inter = jnp.exp(m + lf - m_t)
        dm = jnp.exp(a + (lf - m_t))
        w = jnp.sum(q * k, axis=1, keepdims=True) * dm
        num = inter * jnp.sum(C * q_col, axis=0, keepdims=True) + w * v
        den = inter * jnp.sum(q * n, axis=1, keepdims=True) + w
        hh = num / jnp.maximum(jnp.abs(den), jnp.exp(-m_t))
        wk = jnp.exp(ii + lf - lf - m_t)
        c1_ref[h] = inter * C + (wk * k_col) * v
        n1_ref[h:h + 1, :] = inter * n + wk * k
        m1_ref[:, h:h + 1] = m_t
        y = _rms(hh, gout_ref[:, hs]) * _sigmoid(og_ref[:, hs])
        hm_ref[:, hs] = y


def mlstm_sample(meta, qkvm, gates, b_i, b_f, tail, g_out, st_c, st_n, st_m):
    Bd = qkvm.shape[0]
    bias = jnp.concatenate([b_i, b_f], axis=1)
    bc = jnp.pad(bias, ((0, 0), (0, LANES - 2 * M_HEADS))).reshape(DEPTH, 1, LANES)
    tw = tail.shape[1]
    hm, c1, n1, m1 = _pcall(
        _mlstm_sample_body, (Bd,),
        [pl.BlockSpec((None, 1, 3 * M_WIDTH), lambda b, m: (b, 0, 0)),
         pl.BlockSpec((None, 1, LANES), lambda b, m: (b, 0, 0)),
         pl.BlockSpec((None, 1, LANES), lambda b, m: (m[0], 0, 0)),
         pl.BlockSpec((None, 1, M_WIDTH), lambda b, m: (b, 0, 0)),
         pl.BlockSpec((None, 1, M_WIDTH), lambda b, m: (m[0], 0, 0)),
         pl.BlockSpec((None, None, M_HEADS, M_HD, M_HD), lambda b, m: (m[0], b, 0, 0, 0)),
         pl.BlockSpec((None, None, M_HEADS, M_HD), lambda b, m: (m[0], b, 0, 0)),
         pl.BlockSpec((None, None, 1, M_HEADS), lambda b, m: (m[0], b, 0, 0))],
        [pl.BlockSpec((None, 1, M_WIDTH), lambda b, m: (b, 0, 0)),
         pl.BlockSpec((None, M_HEADS, M_HD, M_HD), lambda b, m: (b, 0, 0, 0)),
         pl.BlockSpec((None, M_HEADS, M_HD), lambda b, m: (b, 0, 0)),
         pl.BlockSpec((None, 1, M_HEADS), lambda b, m: (b, 0, 0))],
        [jax.ShapeDtypeStruct((Bd, 1, M_WIDTH), F32),
         jax.ShapeDtypeStruct((Bd, M_HEADS, M_HD, M_HD), F32),
         jax.ShapeDtypeStruct((Bd, M_HEADS, M_HD), F32),
         jax.ShapeDtypeStruct((Bd, 1, M_HEADS), F32)],
        sem=("parallel",), name="mlstm_sample")(
            meta, qkvm.reshape(Bd, 1, 3 * M_WIDTH), gates.reshape(Bd, 1, LANES), bc,
            tail.reshape(Bd, 1, tw), g_out.reshape(DEPTH, 1, M_WIDTH),
            st_c, st_n, st_m.reshape(DEPTH, Bd, 1, M_HEADS))
    return hm.reshape(Bd, M_WIDTH), c1, n1, m1.reshape(Bd, M_HEADS)


def _merge_body(meta, att_ref, hm_ref, wa_ref, wm_ref, ga_ref, gb_ref, o_ref, *, precise):
    ya = _matmul_tile(att_ref, wa_ref, precise)
    ym = _matmul_tile(hm_ref, wm_ref, precise)
    out = _sigmoid(ga_ref[...].astype(F32)) * ya + _sigmoid(gb_ref[...].astype(F32)) * ym
    o_ref[...] = out.astype(o_ref.dtype)


def merge(meta, att, hm, w_att_out, w_mlstm_out, tail, out_dtype, precise, tn=512):
    T = att.shape[0]
    tm = _m_tile(T)
    ga_off = M_WIDTH // tn
    gb_off = (M_WIDTH + D_MODEL) // tn
    return _pcall(
        functools.partial(_merge_body, precise=precise), (T // tm, D_MODEL // tn),
        [pl.BlockSpec((tm, ATT_OUT), lambda i, j, m: (i, 0)),
         pl.BlockSpec((tm, M_WIDTH), lambda i, j, m: (i, 0)),
         pl.BlockSpec((None, ATT_OUT, tn), lambda i, j, m: (m[0], 0, j)),
         pl.BlockSpec((None, M_WIDTH, tn), lambda i, j, m: (m[0], 0, j)),
         pl.BlockSpec((tm, tn), lambda i, j, m: (i, ga_off + j)),
         pl.BlockSpec((tm, tn), lambda i, j, m: (i, gb_off + j))],
        pl.BlockSpec((tm, tn), lambda i, j, m: (i, j)),
        jax.ShapeDtypeStruct((T, D_MODEL), out_dtype),
        sem=("parallel", "arbitrary"), name="merge")(meta, att, hm, w_att_out, w_mlstm_out, tail, tail)


def _top2_sum(a, b, c, d):
    hi1, lo1 = jnp.maximum(a, b), jnp.minimum(a, b)
    hi2, lo2 = jnp.maximum(c, d), jnp.minimum(c, d)
    return jnp.maximum(hi1, hi2) + jnp.maximum(jnp.minimum(hi1, hi2), jnp.maximum(lo1, lo2))


def _route_rows(s, sb):
    rows = [sb[e:e + 1, :] for e in range(N_EXPERTS)]
    urows = [s[e:e + 1, :] for e in range(N_EXPERTS)]
    gs = [_top2_sum(*rows[g * EPG:(g + 1) * EPG]) for g in range(N_EGROUPS)]
    best = gs[0]
    gsel = jnp.zeros(best.shape, I32)
    for g in range(1, N_EGROUPS):
        upd = gs[g] > best
        gsel = jnp.where(upd, g, gsel)
        best = jnp.where(upd, gs[g], best)
    vals, uvals = [], []
    for i in range(EPG):
        v, u = rows[i], urows[i]
        for g in range(1, N_EGROUPS):
            v = jnp.where(gsel == g, rows[g * EPG + i], v)
            u = jnp.where(gsel == g, urows[g * EPG + i], u)
        vals.append(v)
        uvals.append(u)
    b1, i1, u1 = vals[0], jnp.zeros(best.shape, I32), uvals[0]
    for i in range(1, EPG):
        upd = vals[i] > b1
        i1 = jnp.where(upd, i, i1)
        u1 = jnp.where(upd, uvals[i], u1)
        b1 = jnp.where(upd, vals[i], b1)
    b2 = jnp.full(best.shape, -jnp.inf, F32)
    i2 = jnp.zeros(best.shape, I32)
    u2 = jnp.zeros(best.shape, F32)
    for i in range(EPG):
        upd = (i1 != i) & (vals[i] > b2)
        i2 = jnp.where(upd, i, i2)
        u2 = jnp.where(upd, uvals[i], u2)
        b2 = jnp.where(upd, vals[i], b2)
    tot = u1 + u2
    eidx = jnp.concatenate([gsel * EPG + i1, gsel * EPG + i2], axis=0)
    wts = jnp.concatenate([u1 / tot, u2 / tot], axis=0)
    return eidx, wts


def _norm_route_body(meta, x_ref, g_ref, sc_ref, sh_ref, wr_ref, br_ref, hf_ref, e_ref, w_ref, *, on_mxu):
    hf = _rms(x_ref[...], g_ref[...]) * (1.0 + sc_ref[...]) + sh_ref[...]
    hf_ref[...] = hf.astype(hf_ref.dtype)
    wr = wr_ref[...]
    if on_mxu:
        h_hi, h_lo = _split(hf)
        w_hi, w_lo = _split(wr)
        logits = _dot_nt(w_hi, h_hi) + (_dot_nt(w_hi, h_lo) + _dot_nt(w_lo, h_hi))
    else:
        logits = jnp.sum(wr * hf, axis=1, keepdims=True)
    s = _sigmoid(logits)
    eidx, wts = _route_rows(s, s + br_ref[...])
    e_ref[...] = eidx
    w_ref[...] = wts


def norm_route(meta, x3, g_all, sc, sh, wr_t, b_router, hf_dtype):
    B, S, D = x3.shape
    ts = _row_tile(S)
    hf, eidx, wts = _pcall(
        functools.partial(_norm_route_body, on_mxu=ts >= LANES), (B, S // ts),
        [pl.BlockSpec((None, ts, D), lambda b, i, m: (b, i, 0)),
         pl.BlockSpec((None, 1, D), lambda b, i, m: (m[0], 0, 0)),
         pl.BlockSpec((None, 1, D), lambda b, i, m: (b, 0, 0)),
         pl.BlockSpec((None, 1, D), lambda b, i, m: (b, 0, 0)),
         pl.BlockSpec((N_EXPERTS, D), lambda b, i, m: (0, 0)),
         pl.BlockSpec((N_EXPERTS, 1), lambda b, i, m: (0, 0))],
        [pl.BlockSpec((None, ts, D), lambda b, i, m: (b, i, 0)),
         pl.BlockSpec((None, TOP_K, ts), lambda b, i, m: (b, 0, i)),
         pl.BlockSpec((None, TOP_K, ts), lambda b, i, m: (b, 0, i))],
        [jax.ShapeDtypeStruct((B, S, D), hf_dtype),
         jax.ShapeDtypeStruct((B, TOP_K, S), I32),
         jax.ShapeDtypeStruct((B, TOP_K, S), F32)],
        sem=("parallel", "parallel"), name="norm_route")(
            meta, x3, g_all, sc, sh, wr_t, b_router.reshape(N_EXPERTS, 1))
    T = B * S
    eidx = jnp.swapaxes(eidx, 1, 2).reshape(T, TOP_K)
    wts = jnp.swapaxes(wts, 1, 2).reshape(T, TOP_K)
    return hf.reshape(T, D), eidx, wts


def _expert_body(meta, x_ref, rw_ref, wg_ref, wu_ref, wd_ref, o_ref, *scratch, precise):
    blk = pl.program_id(0)
    n_used = meta[1]
    e = meta[2 + blk]
    e_prev = meta[2 + jnp.maximum(blk - 1, 0)]
    refs = (wg_ref, wu_ref, wd_ref)

    @pl.when((blk == 0) | (e != e_prev))
    def _():
        for i, r in enumerate(refs):
            if precise:
                hi, lo = _split(r[...])
                scratch[2 * i][...] = hi
                scratch[2 * i + 1][...] = lo
            else:
                scratch[i][...] = r[...].astype(BF16)

    @pl.when(blk < n_used)
    def _():
        if precise:
            x_hi, x_lo = _split(x_ref[...])
            h1 = _dot3(x_hi, x_lo, scratch[0][...], scratch[1][...])
            h2 = _dot3(x_hi, x_lo, scratch[2][...], scratch[3][...])
            a_hi, a_lo = _split(h1 * _sigmoid(h1) * h2)
            y = _dot3(a_hi, a_lo, scratch[4][...], scratch[5][...])
        else:
            x = x_ref[...]
            h1 = _dot(x, scratch[0][...])
            h2 = _dot(x, scratch[1][...])
            y = _dot((h1 * _sigmoid(h1) * h2).astype(BF16), scratch[2][...])
        o_ref[...] = y * rw_ref[...]

    @pl.when(blk >= n_used)
    def _():
        o_ref[...] = jnp.zeros_like(o_ref)


def moe(layer, hf, eidx, wts, w_gate, w_up, w_down, bm, precise):
    T, D = hf.shape
    A = T * TOP_K
    n_blocks = -(-A // bm) + N_EXPERTS
    rows = n_blocks * bm
    e_flat = eidx.reshape(A)
    onehot = (e_flat[:, None] == jnp.arange(N_EXPERTS, dtype=I32)[None, :]).astype(I32)
    csum = jnp.cumsum(onehot, axis=0)
    counts = csum[-1]
    rank = jnp.sum((csum - onehot) * onehot, axis=1)
    padded = (counts + bm - 1) // bm * bm
    pad_end = jnp.cumsum(padded)
    pad_start = pad_end - padded
    pos = jnp.sum(pad_start[None, :] * onehot, axis=1) + rank
    row_tok = jnp.full((rows,), T, I32).at[pos].set(jnp.arange(A, dtype=I32) // TOP_K)
    row_w = jnp.zeros((rows,), F32).at[pos].set(wts.reshape(A))
    xs = jnp.concatenate([hf, jnp.zeros((1, D), hf.dtype)], axis=0)[row_tok]
    n_used = pad_end[-1] // bm
    blk_ids = jnp.arange(n_blocks, dtype=I32)
    blk_exp = jnp.minimum(jnp.sum((pad_end[None, :] <= (blk_ids * bm)[:, None]).astype(I32), axis=1), N_EXPERTS - 1)
    last_exp = blk_exp[jnp.maximum(n_used - 1, 0)]
    blk_exp = jnp.where(blk_ids < n_used, blk_exp, last_exp)
    meta = jnp.concatenate([layer.reshape(1), n_used.reshape(1).astype(I32), blk_exp])
    n_scr = 6 if precise else 3
    scr = []
    for shape in ((D, D_EXPERT), (D, D_EXPERT), (D_EXPERT, D)):
        scr += [pltpu.VMEM(shape, BF16)] * (n_scr // 3)
    ys = _pcall(
        functools.partial(_expert_body, precise=precise), (n_blocks,),
        [pl.BlockSpec((bm, D), lambda b, m: (b, 0)),
         pl.BlockSpec((bm, 1), lambda b, m: (b, 0)),
         pl.BlockSpec((None, None, D, D_EXPERT), lambda b, m: (m[0], m[2 + b], 0, 0)),
         pl.BlockSpec((None, None, D, D_EXPERT), lambda b, m: (m[0], m[2 + b], 0, 0)),
         pl.BlockSpec((None, None, D_EXPERT, D), lambda b, m: (m[0], m[2 + b], 0, 0))],
        pl.BlockSpec((bm, D), lambda b, m: (b, 0)),
        jax.ShapeDtypeStruct((rows, D), F32),
        scratch=scr, sem=("arbitrary",), name="experts")(
            meta, xs, row_w.reshape(rows, 1), w_gate, w_up, w_down)
    return ys[pos.reshape(T, TOP_K).T.reshape(A)].reshape(TOP_K, T, D)


def _moe_residual_body(meta, x_ref, y0_ref, y1_ref, gt_ref, o_ref):
    o_ref[...] = x_ref[...] + gt_ref[...] * (y0_ref[...] + y1_ref[...])


def moe_residual(x3, y2, gt):
    B, S, D = x3.shape
    ts = _row_tile(S)
    y3 = y2.reshape(TOP_K, B, S, D)
    meta = jnp.zeros((1,), I32)
    return _pcall(
        _moe_residual_body, (B, S // ts),
        [pl.BlockSpec((None, ts, D), lambda b, i, m: (b, i, 0)),
         pl.BlockSpec((None, None, ts, D), lambda b, i, m: (0, b, i, 0)),
         pl.BlockSpec((None, None, ts, D), lambda b, i, m: (1, b, i, 0)),
         pl.BlockSpec((None, 1, D), lambda b, i, m: (b, 0, 0))],
        pl.BlockSpec((None, ts, D), lambda b, i, m: (b, i, 0)),
        jax.ShapeDtypeStruct((B, S, D), F32),
        sem=("parallel", "parallel"), name="moe_residual")(meta, x3, y3, y3, gt)


def _rope_tables(pos):
    half = HEAD_DIM // 2
    inv = ROPE_THETA ** (-jnp.arange(half, dtype=F32) / half)
    ang = pos.astype(F32)[:, None] * inv[None, :]
    cos, sin = jnp.cos(ang), jnp.sin(ang)
    return jnp.concatenate([cos, cos], axis=1), jnp.concatenate([-sin, sin], axis=1)


def _layer(layer, x3, mod, params, rope, sample_state):
    (g_mix, g_ffn, w_in, w_gates, w_tail, b_igate, b_fgate, g_mlstm_out, w_att_out, w_mlstm_out,
     w_mix_out, wr_t, b_router, w_exp_gate, w_exp_up, w_exp_down) = params
    B, S, D = x3.shape
    T = B * S
    sample = sample_state is not None
    act = F32 if sample else BF16
    meta = layer.reshape(1)
    sh1, sc1, gt1, sh2, sc2, gt2 = [m.reshape(B, 1, D) for m in jnp.split(mod, 6, axis=-1)]

    wins = []
    if sample:
        h = norm_mod(meta, x3, g_mix, sc1, sh1, act).reshape(T, D)
        qkv = linear(meta, h, w_in, 0, 3 * ATT_WIDTH, F32, True, rot=(rope[0], rope[1], 2 * ATT_WIDTH // 512))
        for g in range(len(ATT_GROUPS)):
            kg = qkv[:, ATT_WIDTH + g * ATT_OUT:ATT_WIDTH + (g + 1) * ATT_OUT]
            vg = qkv[:, 2 * ATT_WIDTH + g * ATT_OUT:2 * ATT_WIDTH + (g + 1) * ATT_OUT]
            wins.append(jnp.concatenate([kg, vg], axis=1).reshape(B, S, 2, ATT_HPG, HEAD_DIM))
    else:
        hs = norm_mod_streams(meta, x3, g_mix, sc1, sh1)
        h = hs[0]
        qkv_g = [linear(meta, hs[g], w_in, g * ATT_OUT, 3 * ATT_OUT, F32, False, rot=(rope[g][0], rope[g][1], 2),
                        col_step=len(ATT_GROUPS)) for g in range(len(ATT_GROUPS))]
        for g, (win, dil) in enumerate(ATT_GROUPS):
            keep = min(win, S)
            L = S // dil
            kv = qkv_g[g].reshape(B, dil, L, 3 * ATT_OUT)[:, :, L - keep // dil:, ATT_OUT:]
            wins.append(jnp.swapaxes(kv, 1, 2).reshape(B, keep, 2, ATT_HPG, HEAD_DIM))
    qkvm = linear(meta, h, w_in, 3 * ATT_WIDTH, 3 * M_WIDTH, act, sample)
    gates = linear(meta, h, w_gates, 0, LANES, F32, sample, tn=LANES)
    tail = linear(meta, h, w_tail, 0, M_WIDTH + 2 * D_MODEL, act, sample)

    if sample:
        caches, st_c, st_n, st_m = sample_state
        att = sample_attention(meta, qkv, caches)
        hm, c1, n1, m1 = mlstm_sample(meta, qkvm, gates, b_igate, b_fgate, tail, g_mlstm_out, st_c, st_n, st_m)
    else:
        outs, lses = zip(*[band_attention(qkv_g[g], B, S, g) for g in range(len(ATT_GROUPS))])
        att = attn_merge(outs, lses)
        hm, c1, n1, m1 = mlstm_prompt(meta, qkvm, gates, b_igate, b_fgate, tail, g_mlstm_out, B, S)

    merged = merge(meta, att, hm, w_att_out, w_mlstm_out, tail, act, sample)
    x2 = linear_residual(meta, merged, w_mix_out, x3.reshape(T, D), gt1, S, sample)
    x3 = x2.reshape(B, S, D)

    hf, eidx, wts = norm_route(meta, x3, g_ffn, sc2, sh2, wr_t, b_router, act)
    bm = MOE_BLOCK_SAMPLE if sample else MOE_BLOCK_PROMPT
    y2 = moe(layer, hf, eidx, wts, w_exp_gate, w_exp_up, w_exp_down, bm, sample)
    x3 = moe_residual(x3, y2, gt2)
    return x3, (wins[0], wins[1], wins[2], c1, n1, m1)


def kernel(x_prompt, x_sample, cache_win_w128, cache_win_w512, cache_win_w2048, state_mlstm_c, state_mlstm_n,
           state_mlstm_m, c_prompt, c_sample, w_ada, b_ada, g_mix, g_ffn, w_in, b_igate, b_fgate, g_mlstm_out,
           w_att_out, w_mlstm_out, w_mix_out, w_router, b_router, w_exp_gate, w_exp_up, w_exp_down, g_final):
    B, S, D = x_prompt.shape
    Bd, Sd, _ = x_sample.shape
    assert Sd == 1 and D == D_MODEL and w_in.shape[2] == D_IN

    n_c = B + Bd
    c_rows = -(-n_c // SUBLANES) * SUBLANES
    c_all = jnp.pad(jnp.concatenate([c_prompt, c_sample], axis=0), ((0, c_rows - n_c), (0, 0)))
    mod = ada_modulation(c_all, w_ada, b_ada)

    w_gates = jnp.pad(w_in[:, :, GATE_OFF:TAIL_OFF], ((0, 0), (0, 0), (0, LANES - 2 * M_HEADS)))
    w_tail = w_in[:, :, TAIL_OFF:]
    params = (g_mix.reshape(DEPTH, 1, D), g_ffn.reshape(DEPTH, 1, D), w_in, w_gates, w_tail, b_igate, b_fgate,
              g_mlstm_out, w_att_out, w_mlstm_out, w_mix_out, w_router.T, b_router,
              w_exp_gate, w_exp_up, w_exp_down)
    rope_p = [_rope_tables((jnp.arange(S // dil, dtype=I32)[None, :] * dil
                            + jnp.arange(dil, dtype=I32)[:, None]).reshape(S)) for _, dil in ATT_GROUPS]
    rope_s = _rope_tables(jnp.full((Bd,), PAST_LEN, I32))
    sample_state = ((cache_win_w128, cache_win_w512, cache_win_w2048), state_mlstm_c, state_mlstm_n, state_mlstm_m)

    def body(carry, layer):
        xp, xs = carry
        mod_l = lax.dynamic_index_in_dim(mod, layer, 0, keepdims=False)
        xp, out_p = _layer(layer, xp, mod_l[:B], params, rope_p, None)
        xs, out_s = _layer(layer, xs, mod_l[B:n_c], params, rope_s, sample_state)
        return (xp, xs), (out_p, out_s)

    (xp, xs), (out_p, out_s) = lax.scan(body, (x_prompt, x_sample), jnp.arange(DEPTH, dtype=I32))
    y_prompt = norm_final(xp, g_final)
    y_sample = norm_final(xs, g_final)
    return (y_prompt, y_sample) + tuple(out_p) + tuple(out_s)
```

```python
import functools

import numpy as np
import jax
import jax.numpy as jnp
from jax import lax
from jax.experimental import pallas as pl
from jax.experimental.pallas import tpu as pltpu

F32 = jnp.float32
BF16 = jnp.bfloat16
I32 = jnp.int32

D_MODEL = 2048
DEPTH = 4
PAST_LEN = 16384
HEAD_DIM = 128
ATT_GROUPS = ((128, 1), (512, 4), (2048, 16))
ATT_HPG = 4
ATT_WIDTH = len(ATT_GROUPS) * ATT_HPG * HEAD_DIM
ATT_OUT = ATT_HPG * HEAD_DIM
ATT_BLOCK = 128
ROPE_THETA = 10000.0
M_HEADS = 4
M_WIDTH = D_MODEL // 2
M_HD = M_WIDTH // M_HEADS
N_EXPERTS = 16
N_EGROUPS = 4
EPG = N_EXPERTS // N_EGROUPS
TOP_K = 2
D_EXPERT = D_MODEL // 4
EPS = 1e-6
NEG_INF = -1e30
D_IN = 3 * ATT_WIDTH + 3 * M_WIDTH + 2 * M_HEADS + M_WIDTH + 2 * D_MODEL
GATE_OFF = 3 * ATT_WIDTH + 3 * M_WIDTH
TAIL_OFF = GATE_OFF + 2 * M_HEADS

V7X_VMEM_BYTES = 64 * 1024 * 1024
LANES = 128
SUBLANES = 8
VMEM_LIMIT = 56 * 1024 * 1024

MLSTM_CHUNK = 256
MOE_BLOCK_PROMPT = 256
MOE_BLOCK_SAMPLE = 8


def _pcall(body, grid, in_specs, out_specs, out_shape, scratch=(), sem=None, name=None):
    return pl.pallas_call(
        body,
        grid_spec=pltpu.PrefetchScalarGridSpec(
            num_scalar_prefetch=1, grid=grid, in_specs=in_specs, out_specs=out_specs,
            scratch_shapes=list(scratch)),
        out_shape=out_shape,
        compiler_params=pltpu.CompilerParams(dimension_semantics=sem, vmem_limit_bytes=VMEM_LIMIT),
        name=name)


def _split(x):
    hi = x.astype(BF16)
    lo = (x - hi.astype(F32)).astype(BF16)
    return hi, lo


def _dot(a, b):
    return jnp.dot(a, b, preferred_element_type=F32)


def _dot_nt(a, b):
    return lax.dot_general(a, b, (((1,), (1,)), ((), ())), preferred_element_type=F32)


def _dot_tn(a, b):
    return lax.dot_general(a, b, (((0,), (0,)), ((), ())), preferred_element_type=F32)


def _dot3(a_hi, a_lo, w_hi, w_lo):
    return _dot(a_hi, w_hi) + (_dot(a_hi, w_lo) + _dot(a_lo, w_hi))


def _sigmoid(x):
    return 1.0 / (1.0 + jnp.exp(-x))


def _log_sigmoid(x):
    return jnp.minimum(x, 0.0) - jnp.log1p(jnp.exp(-jnp.abs(x)))


def _ada_body(meta, c_ref, w_ref, b_ref, o_ref):
    c = c_ref[...]
    a_hi, a_lo = _split(c * _sigmoid(c))
    w_hi, w_lo = _split(w_ref[...])
    o_ref[...] = _dot3(a_hi, a_lo, w_hi, w_lo) + b_ref[...]


def ada_modulation(c_all, w_ada, b_ada):
    rows = c_all.shape[0]
    n_out = w_ada.shape[2]
    tn = 1024
    meta = jnp.zeros((1,), I32)
    return _pcall(
        _ada_body, (DEPTH, n_out // tn),
        [pl.BlockSpec((rows, D_MODEL), lambda l, j, m: (0, 0)),
         pl.BlockSpec((None, D_MODEL, tn), lambda l, j, m: (l, 0, j)),
         pl.BlockSpec((None, 1, tn), lambda l, j, m: (l, 0, j))],
        pl.BlockSpec((None, rows, tn), lambda l, j, m: (l, 0, j)),
        jax.ShapeDtypeStruct((DEPTH, rows, n_out), F32),
        sem=("parallel", "parallel"), name="ada_mod")(meta, c_all, w_ada, b_ada.reshape(DEPTH, 1, n_out))


def _rms(x, g):
    return x * lax.rsqrt(jnp.mean(x * x, axis=-1, keepdims=True) + EPS) * g


def _norm_mod_body(meta, x_ref, g_ref, sc_ref, sh_ref, o_ref):
    y = _rms(x_ref[...], g_ref[...])
    o_ref[...] = (y * (1.0 + sc_ref[...]) + sh_ref[...]).astype(o_ref.dtype)


def _norm_plain_body(meta, x_ref, g_ref, o_ref):
    o_ref[...] = _rms(x_ref[...], g_ref[...]).astype(o_ref.dtype)


def _row_tile(S):
    return min(S, 512)


def norm_mod(meta, x3, g_all, sc, sh, out_dtype):
    B, S, D = x3.shape
    ts = _row_tile(S)
    return _pcall(
        _norm_mod_body, (B, S // ts),
        [pl.BlockSpec((None, ts, D), lambda b, i, m: (b, i, 0)),
         pl.BlockSpec((None, 1, D), lambda b, i, m: (m[0], 0, 0)),
         pl.BlockSpec((None, 1, D), lambda b, i, m: (b, 0, 0)),
         pl.BlockSpec((None, 1, D), lambda b, i, m: (b, 0, 0))],
        pl.BlockSpec((None, ts, D), lambda b, i, m: (b, i, 0)),
        jax.ShapeDtypeStruct((B, S, D), out_dtype),
        sem=("parallel", "parallel"), name="norm_mod")(meta, x3, g_all, sc, sh)


def _norm_streams_body(meta, x_ref, g_ref, sc_ref, sh_ref, o0_ref, o1_ref, o2_ref, y_s, *, ts):
    y = _rms(x_ref[...], g_ref[...]) * (1.0 + sc_ref[...]) + sh_ref[...]
    o0_ref[...] = y.astype(o0_ref.dtype)
    n_slabs = y.shape[1] // LANES
    for c in range(n_slabs):
        y_s[c] = y[:, c * LANES:(c + 1) * LANES]
    for o_ref, (_, dil) in ((o1_ref, ATT_GROUPS[1]), (o2_ref, ATT_GROUPS[2])):
        for r in range(dil):
            for c in range(n_slabs):
                o_ref[r, :, c * LANES:(c + 1) * LANES] = (
                    y_s[c, pl.ds(r, ts // dil, stride=dil), :].astype(o_ref.dtype))


def norm_mod_streams(meta, x3, g_all, sc, sh):
    B, S, D = x3.shape
    ts = _row_tile(S)
    d1, d2 = ATT_GROUPS[1][1], ATT_GROUPS[2][1]
    outs = _pcall(
        functools.partial(_norm_streams_body, ts=ts), (B, S // ts),
        [pl.BlockSpec((None, ts, D), lambda b, i, m: (b, i, 0)),
         pl.BlockSpec((None, 1, D), lambda b, i, m: (m[0], 0, 0)),
         pl.BlockSpec((None, 1, D), lambda b, i, m: (b, 0, 0)),
         pl.BlockSpec((None, 1, D), lambda b, i, m: (b, 0, 0))],
        [pl.BlockSpec((None, ts, D), lambda b, i, m: (b, i, 0)),
         pl.BlockSpec((None, d1, ts // d1, D), lambda b, i, m: (b, 0, i, 0)),
         pl.BlockSpec((None, d2, ts // d2, D), lambda b, i, m: (b, 0, i, 0))],
        [jax.ShapeDtypeStruct((B, S, D), BF16),
         jax.ShapeDtypeStruct((B, d1, S // d1, D), BF16),
         jax.ShapeDtypeStruct((B, d2, S // d2, D), BF16)],
        scratch=[pltpu.VMEM((D // LANES, ts, LANES), F32)],
        sem=("parallel", "parallel"), name="norm_mod_streams")(meta, x3, g_all, sc, sh)
    return [o.reshape(B * S, D) for o in outs]


def norm_final(x3, g):
    B, S, D = x3.shape
    ts = _row_tile(S)
    meta = jnp.zeros((1,), I32)
    return _pcall(
        _norm_plain_body, (B, S // ts),
        [pl.BlockSpec((None, ts, D), lambda b, i, m: (b, i, 0)),
         pl.BlockSpec((1, D), lambda b, i, m: (0, 0))],
        pl.BlockSpec((None, ts, D), lambda b, i, m: (b, i, 0)),
        jax.ShapeDtypeStruct((B, S, D), F32),
        sem=("parallel", "parallel"), name="norm_final")(meta, x3, g.reshape(1, D))


def _matmul_tile(a_ref, w_ref, precise):
    w = w_ref[...]
    if precise:
        a_hi, a_lo = _split(a_ref[...])
        w_hi, w_lo = _split(w)
        return _dot3(a_hi, a_lo, w_hi, w_lo)
    return _dot(a_ref[...], w.astype(BF16))


def _linear_body(meta, a_ref, w_ref, o_ref, *, precise):
    o_ref[...] = _matmul_tile(a_ref, w_ref, precise).astype(o_ref.dtype)


def _linear_rot_body(meta, a_ref, w_ref, cos_ref, sin_ref, o_ref, *, precise, n_rot, tn):
    acc = _matmul_tile(a_ref, w_ref, precise)
    j = pl.program_id(1)

    @pl.when(j < n_rot)
    def _():
        c = cos_ref[...]
        s = sin_ref[...]
        for h in range(tn // HEAD_DIM):
            hs = slice(h * HEAD_DIM, (h + 1) * HEAD_DIM)
            x = acc[:, hs]
            o_ref[:, hs] = (x * c + pltpu.roll(x, HEAD_DIM // 2, 1) * s).astype(o_ref.dtype)

    @pl.when(j >= n_rot)
    def _():
        o_ref[...] = acc.astype(o_ref.dtype)


def _linear_res_body(meta, a_ref, w_ref, x_ref, gt_ref, o_ref, *, precise):
    o_ref[...] = x_ref[...] + gt_ref[...] * _matmul_tile(a_ref, w_ref, precise)


def _m_tile(T):
    return min(T, 1024)


def linear(meta, a, w, col_off, n_out, out_dtype, precise, tn=512, rot=None, col_step=1):
    T, K = a.shape
    tm = _m_tile(T)
    off = col_off // tn
    in_specs = [pl.BlockSpec((tm, K), lambda i, j, m: (i, 0)),
                pl.BlockSpec((None, K, tn), lambda i, j, m: (m[0], 0, off + j * col_step))]
    args = [a, w]
    if rot is None:
        body = functools.partial(_linear_body, precise=precise)
    else:
        cos_t, sin_t, n_rot = rot
        nper = cos_t.shape[0] // tm
        in_specs += [pl.BlockSpec((tm, HEAD_DIM), lambda i, j, m: (i % nper, 0)),
                     pl.BlockSpec((tm, HEAD_DIM), lambda i, j, m: (i % nper, 0))]
        args += [cos_t, sin_t]
        body = functools.partial(_linear_rot_body, precise=precise, n_rot=n_rot, tn=tn)
    return _pcall(
        body, (T // tm, n_out // tn), in_specs,
        pl.BlockSpec((tm, tn), lambda i, j, m: (i, j)),
        jax.ShapeDtypeStruct((T, n_out), out_dtype),
        sem=("parallel", "arbitrary"), name="linear")(meta, *args)


def linear_residual(meta, a, w, x2, gt, rows_per_batch, precise, tn=512):
    T, K = a.shape
    D = x2.shape[1]
    tm = _m_tile(T)
    if rows_per_batch == 1:
        gt_arr = gt.reshape(T, D)
        gt_spec = pl.BlockSpec((tm, tn), lambda i, j, m: (i, j))
    else:
        per = rows_per_batch // tm
        gt_arr = gt
        gt_spec = pl.BlockSpec((None, 1, tn), lambda i, j, m: (i // per, 0, j))
    return _pcall(
        functools.partial(_linear_res_body, precise=precise), (T // tm, D // tn),
        [pl.BlockSpec((tm, K), lambda i, j, m: (i, 0)),
         pl.BlockSpec((None, K, tn), lambda i, j, m: (m[0], 0, j)),
         pl.BlockSpec((tm, tn), lambda i, j, m: (i, j)),
         gt_spec],
        pl.BlockSpec((tm, tn), lambda i, j, m: (i, j)),
        jax.ShapeDtypeStruct((T, D), F32),
        sem=("parallel", "arbitrary"), name="linear_residual")(meta, a, w, x2, gt_arr)


def _attn_body(meta, q_ref, k_ref, v_ref, o_ref, lse_ref, *, nb):
    scale = HEAD_DIM ** -0.5
    qi = lax.broadcasted_iota(I32, (ATT_BLOCK, ATT_BLOCK), 0)
    kj = lax.broadcasted_iota(I32, (ATT_BLOCK, ATT_BLOCK), 1)
    m_cur = kj <= qi
    m_prev = qi <= kj

    def blk(n, carry):
        qs = pl.multiple_of(n * ATT_BLOCK, ATT_BLOCK)
        ps = pl.multiple_of(jnp.maximum(n - 1, 0) * ATT_BLOCK, ATT_BLOCK)
        prev_cap = jnp.where(n > 0, -NEG_INF, NEG_INF)
        for h in range(ATT_HPG):
            hs = slice(h * HEAD_DIM, (h + 1) * HEAD_DIM)
            q = q_ref[pl.ds(qs, ATT_BLOCK), hs].astype(BF16)
            kc = k_ref[pl.ds(qs, ATT_BLOCK), hs].astype(BF16)
            kp = k_ref[pl.ds(ps, ATT_BLOCK), hs].astype(BF16)
            vc = v_ref[pl.ds(qs, ATT_BLOCK), hs].astype(BF16)
            vp = v_ref[pl.ds(ps, ATT_BLOCK), hs].astype(BF16)
            sc = jnp.where(m_cur, _dot_nt(q, kc) * scale, NEG_INF)
            sp = jnp.minimum(jnp.where(m_prev, _dot_nt(q, kp) * scale, NEG_INF), prev_cap)
            mx = jnp.maximum(jnp.max(sc, axis=1, keepdims=True), jnp.max(sp, axis=1, keepdims=True))
            pc = jnp.exp(sc - mx)
            pp = jnp.exp(sp - mx)
            den = jnp.sum(pc, axis=1, keepdims=True) + jnp.sum(pp, axis=1, keepdims=True)
            o = (_dot(pc.astype(BF16), vc) + _dot(pp.astype(BF16), vp)) / den
            o_ref[pl.ds(qs, ATT_BLOCK), hs] = o.astype(o_ref.dtype)
            lse_ref[pl.ds(qs, ATT_BLOCK), hs] = jnp.broadcast_to(mx + jnp.log(den), (ATT_BLOCK, HEAD_DIM))
        return carry

    lax.fori_loop(0, nb, blk, 0)


def band_attention(qkv_g, B, S, g):
    _, dil = ATT_GROUPS[g]
    L = S // dil
    assert L % ATT_BLOCK == 0
    qv = qkv_g.reshape(B, dil, L, 3 * ATT_OUT)
    meta = jnp.zeros((1,), I32)
    o, lse = _pcall(
        functools.partial(_attn_body, nb=L // ATT_BLOCK), (B, dil),
        [pl.BlockSpec((None, None, L, ATT_OUT), lambda b, r, m: (b, r, 0, 0)),
         pl.BlockSpec((None, None, L, ATT_OUT), lambda b, r, m: (b, r, 0, 1)),
         pl.BlockSpec((None, None, L, ATT_OUT), lambda b, r, m: (b, r, 0, 2))],
        [pl.BlockSpec((None, L, ATT_OUT), lambda b, r, m: (b, 0, r)),
         pl.BlockSpec((None, L, ATT_OUT), lambda b, r, m: (b, 0, r))],
        [jax.ShapeDtypeStruct((B, L, dil * ATT_OUT), BF16),
         jax.ShapeDtypeStruct((B, L, dil * ATT_OUT), F32)],
        sem=("parallel", "parallel"), name="band_attention")(meta, qv, qv, qv)
    return o.reshape(B * S, ATT_OUT), lse.reshape(B * S, ATT_OUT)


def _attn_merge_body(meta, o0, o1, o2, l0, l1, l2, att_ref):
    a0, a1, a2 = l0[...], l1[...], l2[...]
    mx = jnp.maximum(jnp.maximum(a0, a1), a2)
    e0, e1, e2 = jnp.exp(a0 - mx), jnp.exp(a1 - mx), jnp.exp(a2 - mx)
    num = e0 * o0[...].astype(F32) + e1 * o1[...].astype(F32) + e2 * o2[...].astype(F32)
    att_ref[...] = (num / (e0 + e1 + e2)).astype(att_ref.dtype)


def attn_merge(outs, lses):
    T = outs[0].shape[0]
    tm = _m_tile(T)
    spec = pl.BlockSpec((tm, ATT_OUT), lambda i, m: (i, 0))
    meta = jnp.zeros((1,), I32)
    return _pcall(_attn_merge_body, (T // tm,), [spec] * 6, spec,
                  jax.ShapeDtypeStruct((T, ATT_OUT), BF16),
                  sem=("parallel",), name="attn_merge")(meta, *outs, *lses)


def _attn_sample_body(meta, q_ref, c0_ref, c1_ref, c2_ref, att_ref):
    scale = HEAD_DIM ** -0.5
    row = q_ref[...]
    caches = (c0_ref, c1_ref, c2_ref)
    for h in range(ATT_HPG):
        outs, lses = [], []
        for g in range(len(ATT_GROUPS)):
            col = (g * ATT_HPG + h) * HEAD_DIM
            q = row[:, col:col + HEAD_DIM]
            k_new = row[:, ATT_WIDTH + col:ATT_WIDTH + col + HEAD_DIM]
            v_new = row[:, 2 * ATT_WIDTH + col:2 * ATT_WIDTH + col + HEAD_DIM]
            kc = caches[g][:, 0, h, :]
            vc = caches[g][:, 1, h, :]
            s = jnp.sum(kc * q, axis=1, keepdims=True) * scale
            s0 = jnp.sum(k_new * q, axis=1, keepdims=True) * scale
            mx = jnp.maximum(jnp.max(s, axis=0, keepdims=True), s0)
            p = jnp.exp(s - mx)
            p0 = jnp.exp(s0 - mx)
            den = jnp.sum(p, axis=0, keepdims=True) + p0
            outs.append((jnp.sum(vc * p, axis=0, keepdims=True) + p0 * v_new) / den)
            lses.append(mx + jnp.log(den))
        mxl = jnp.maximum(jnp.maximum(lses[0], lses[1]), lses[2])
        es = [jnp.exp(l - mxl) for l in lses]
        num = es[0] * outs[0] + es[1] * outs[1] + es[2] * outs[2]
        att_ref[:, h * HEAD_DIM:(h + 1) * HEAD_DIM] = num / (es[0] + es[1] + es[2])


def sample_attention(meta, qkv, caches):
    Bd = qkv.shape[0]
    wcols = 3 * ATT_WIDTH
    in_specs = [pl.BlockSpec((None, 1, wcols), lambda b, m: (b, 0, 0))]
    args = [qkv.reshape(Bd, 1, wcols)]
    for g, (win, dil) in enumerate(ATT_GROUPS):
        c = caches[g]
        assert c.shape[2] == win and win == ATT_BLOCK * dil
        args.append(c.reshape(DEPTH, Bd, ATT_BLOCK, dil, 2, ATT_HPG, HEAD_DIM))
        in_specs.append(pl.BlockSpec((None, None, ATT_BLOCK, None, 2, ATT_HPG, HEAD_DIM),
                                     lambda b, m: (m[0], b, 0, 0, 0, 0, 0)))
    att = _pcall(
        _attn_sample_body, (Bd,), in_specs,
        pl.BlockSpec((None, 1, ATT_OUT), lambda b, m: (b, 0, 0)),
        jax.ShapeDtypeStruct((Bd, 1, ATT_OUT), F32),
        sem=("parallel",), name="sample_attention")(meta, *args)
    return att.reshape(Bd, ATT_OUT)


def _mlstm_body(meta, q_ref, k_ref, v_ref, gc_ref, gr_ref, bc_ref, br_ref, og_ref, g_ref,
                hm_ref, caug_ref, m_ref, c_s, m_s, *, Lc, nc):
    h = pl.program_id(1)
    c = pl.program_id(2)

    @pl.when(c == 0)
    def _():
        c_s[...] = jnp.zeros_like(c_s)
        m_s[...] = jnp.zeros_like(m_s)

    gcol = gc_ref[...] + bc_ref[...]
    lane = lax.broadcasted_iota(I32, gcol.shape, 1)
    i_col = jnp.sum(jnp.where(lane == h, gcol, 0.0), axis=1, keepdims=True)
    f_col = jnp.sum(jnp.where(lane == h + M_HEADS, gcol, 0.0), axis=1, keepdims=True)
    grow = gr_ref[...] + br_ref[...]
    sub = lax.broadcasted_iota(I32, grow.shape, 0)
    i_row = jnp.sum(jnp.where(sub == h, grow, 0.0), axis=0, keepdims=True)
    f_row = jnp.sum(jnp.where(sub == h + M_HEADS, grow, 0.0), axis=0, keepdims=True)
    lf_col = _log_sigmoid(f_col)
    lf_row = _log_sigmoid(f_row)

    t_i = lax.broadcasted_iota(I32, (Lc, Lc), 0)
    s_i = lax.broadcasted_iota(I32, (Lc, Lc), 1)
    causal = s_i <= t_i
    b_col = jnp.sum(jnp.where(causal, lf_row, 0.0), axis=1, keepdims=True)
    b_row = jnp.sum(jnp.where(t_i <= s_i, lf_col, 0.0), axis=0, keepdims=True)
    a_row = i_row - b_row
    cmax_col = jnp.max(jnp.where(causal, a_row, -jnp.inf), axis=1, keepdims=True)
    m_prev = m_s[0:1, 0:1]
    m_t = b_col + jnp.maximum(m_prev, cmax_col)
    inter = jnp.exp(m_prev + b_col - m_t)
    dmat = jnp.exp(jnp.where(causal, a_row + (b_col - m_t), NEG_INF))

    q = q_ref[...]
    k = k_ref[...]
    v = v_ref[...]
    kscale = M_HD ** -0.5
    one_col = jnp.where(lax.broadcasted_iota(I32, (Lc, LANES), 1) == 0, 1.0, 0.0).astype(BF16)
    v_aug = jnp.concatenate([v, one_col], axis=1)
    w = _dot_nt(q, k) * kscale * dmat
    caug = c_s[...]
    num = inter * _dot(q, caug.astype(BF16)) + _dot(w.astype(BF16), v_aug)
    den = num[:, M_HD:M_HD + 1]
    hh = num[:, :M_HD] / jnp.maximum(jnp.abs(den), jnp.exp(-m_t))

    b_last = jnp.sum(lf_row, axis=1, keepdims=True)
    m_last = b_last + jnp.maximum(m_prev, jnp.max(a_row, axis=1, keepdims=True))
    decay = jnp.exp(m_prev + b_last - m_last)
    wk_col = jnp.exp(i_col + b_last - b_col - m_last)
    kw = (k.astype(F32) * (wk_col * kscale)).astype(BF16)
    c_new = decay * caug + _dot_tn(kw, v_aug)
    c_s[...] = c_new
    m_s[...] = jnp.broadcast_to(m_last, m_s.shape)

    y = _rms(hh, g_ref[...]) * _sigmoid(og_ref[...].astype(F32))
    hm_ref[...] = y.astype(hm_ref.dtype)

    @pl.when(c == nc - 1)
    def _():
        caug_ref[...] = c_new
        m_ref[...] = jnp.broadcast_to(m_last, m_ref.shape)


def mlstm_prompt(meta, qkvm, gates, b_i, b_f, tail, g_out, B, S):
    Lc = min(MLSTM_CHUNK, S)
    nc = S // Lc
    caw = M_HD + LANES
    q3 = qkvm.reshape(B, S, 3 * M_WIDTH)
    g3 = gates.reshape(B, S, LANES)
    gr = jnp.swapaxes(g3[:, :, :SUBLANES], 1, 2)
    bias = jnp.concatenate([b_i, b_f], axis=1)
    bc = jnp.pad(bias, ((0, 0), (0, LANES - 2 * M_HEADS))).reshape(DEPTH, 1, LANES)
    br = bias.reshape(DEPTH, 2 * M_HEADS, 1)
    t3 = tail.reshape(B, S, tail.shape[1])
    hm, caug, mfin = _pcall(
        functools.partial(_mlstm_body, Lc=Lc, nc=nc), (B, M_HEADS, nc),
        [pl.BlockSpec((None, Lc, M_HD), lambda b, h, c, m: (b, c, h)),
         pl.BlockSpec((None, Lc, M_HD), lambda b, h, c, m: (b, c, M_HEADS + h)),
         pl.BlockSpec((None, Lc, M_HD), lambda b, h, c, m: (b, c, 2 * M_HEADS + h)),
         pl.BlockSpec((None, Lc, LANES), lambda b, h, c, m: (b, c, 0)),
         pl.BlockSpec((None, SUBLANES, Lc), lambda b, h, c, m: (b, 0, c)),
         pl.BlockSpec((None, 1, LANES), lambda b, h, c, m: (m[0], 0, 0)),
         pl.BlockSpec((None, 2 * M_HEADS, 1), lambda b, h, c, m: (m[0], 0, 0)),
         pl.BlockSpec((None, Lc, M_HD), lambda b, h, c, m: (b, c, h)),
         pl.BlockSpec((None, 1, M_HD), lambda b, h, c, m: (m[0], 0, h))],
        [pl.BlockSpec((None, Lc, M_HD), lambda b, h, c, m: (b, c, h)),
         pl.BlockSpec((None, None, M_HD, caw), lambda b, h, c, m: (b, h, 0, 0)),
         pl.BlockSpec((None, None, SUBLANES, LANES), lambda b, h, c, m: (b, h, 0, 0))],
        [jax.ShapeDtypeStruct((B, S, M_WIDTH), BF16),
         jax.ShapeDtypeStruct((B, M_HEADS, M_HD, caw), F32),
         jax.ShapeDtypeStruct((B, M_HEADS, SUBLANES, LANES), F32)],
        scratch=[pltpu.VMEM((M_HD, caw), F32), pltpu.VMEM((SUBLANES, LANES), F32)],
        sem=("parallel", "parallel", "arbitrary"), name="mlstm_prompt")(
            meta, q3, q3, q3, g3, gr, bc, br, t3, g_out.reshape(DEPTH, 1, M_WIDTH))
    return (hm.reshape(B * S, M_WIDTH), caug[..., :M_HD], caug[..., M_HD], mfin[:, :, 0, 0])


def _mlstm_sample_body(meta, x_ref, g_ref, bc_ref, og_ref, gout_ref, c0_ref, n0_ref, m0_ref,
                       hm_ref, c1_ref, n1_ref, m1_ref):
    row = x_ref[...]
    gates = g_ref[...] + bc_ref[...]
    eye = lax.broadcasted_iota(I32, (M_HD, M_HD), 0) == lax.broadcasted_iota(I32, (M_HD, M_HD), 1)
    kscale = M_HD ** -0.5
    for h in range(M_HEADS):
        hs = slice(h * M_HD, (h + 1) * M_HD)
        q = row[:, h * M_HD:(h + 1) * M_HD]
        k = row[:, M_WIDTH + h * M_HD:M_WIDTH + (h + 1) * M_HD] * kscale
        v = row[:, 2 * M_WIDTH + h * M_HD:2 * M_WIDTH + (h + 1) * M_HD]
        q_col = jnp.sum(jnp.where(eye, q, 0.0), axis=1, keepdims=True)
        k_col = jnp.sum(jnp.where(eye, k, 0.0), axis=1, keepdims=True)
        ii = gates[:, h:h + 1]
        lf = _log_sigmoid(gates[:, M_HEADS + h:M_HEADS + h + 1])
        C = c0_ref[h]
        n = n0_ref[h:h + 1, :]
        m = m0_ref[:, h:h + 1]
        a = ii - lf
        m_t = lf + jnp.maximum(m, a)
        inter = jnp.exp(m + lf - m_t)
        dm = jnp.exp(a + (lf - m_t))
        w = jnp.sum(q * k, axis=1, keepdims=True) * dm
        num = inter * jnp.sum(C * q_col, axis=0, keepdims=True) + w * v
        den = inter * jnp.sum(q * n, axis=1, keepdims=True) + w
        hh = num / jnp.maximum(jnp.abs(den), jnp.exp(-m_t))
        wk = jnp.exp(ii + lf - lf - m_t)
        c1_ref[h] = inter * C + (wk * k_col) * v
        n1_ref[h:h + 1, :] = inter * n + wk * k
        m1_ref[:, h:h + 1] = m_t
        y = _rms(hh, gout_ref[:, hs]) * _sigmoid(og_ref[:, hs])
        hm_ref[:, hs] = y


def mlstm_sample(meta, qkvm, gates, b_i, b_f, tail, g_out, st_c, st_n, st_m):
    Bd = qkvm.shape[0]
    bias = jnp.concatenate([b_i, b_f], axis=1)
    bc = jnp.pad(bias, ((0, 0), (0, LANES - 2 * M_HEADS))).reshape(DEPTH, 1, LANES)
    tw = tail.shape[1]
    hm, c1, n1, m1 = _pcall(
        _mlstm_sample_body, (Bd,),
        [pl.BlockSpec((None, 1, 3 * M_WIDTH), lambda b, m: (b, 0, 0)),
         pl.BlockSpec((None, 1, LANES), lambda b, m: (b, 0, 0)),
         pl.BlockSpec((None, 1, LANES), lambda b, m: (m[0], 0, 0)),
         pl.BlockSpec((None, 1, M_WIDTH), lambda b, m: (b, 0, 0)),
         pl.BlockSpec((None, 1, M_WIDTH), lambda b, m: (m[0], 0, 0)),
         pl.BlockSpec((None, None, M_HEADS, M_HD, M_HD), lambda b, m: (m[0], b, 0, 0, 0)),
         pl.BlockSpec((None, None, M_HEADS, M_HD), lambda b, m: (m[0], b, 0, 0)),
         pl.BlockSpec((None, None, 1, M_HEADS), lambda b, m: (m[0], b, 0, 0))],
        [pl.BlockSpec((None, 1, M_WIDTH), lambda b, m: (b, 0, 0)),
         pl.BlockSpec((None, M_HEADS, M_HD, M_HD), lambda b, m: (b, 0, 0, 0)),
         pl.BlockSpec((None, M_HEADS, M_HD), lambda b, m: (b, 0, 0)),
         pl.BlockSpec((None, 1, M_HEADS), lambda b, m: (b, 0, 0))],
        [jax.ShapeDtypeStruct((Bd, 1, M_WIDTH), F32),
         jax.ShapeDtypeStruct((Bd, M_HEADS, M_HD, M_HD), F32),
         jax.ShapeDtypeStruct((Bd, M_HEADS, M_HD), F32),
         jax.ShapeDtypeStruct((Bd, 1, M_HEADS), F32)],
        sem=("parallel",), name="mlstm_sample")(
            meta, qkvm.reshape(Bd, 1, 3 * M_WIDTH), gates.reshape(Bd, 1, LANES), bc,
            tail.reshape(Bd, 1, tw), g_out.reshape(DEPTH, 1, M_WIDTH),
            st_c, st_n, st_m.reshape(DEPTH, Bd, 1, M_HEADS))
    return hm.reshape(Bd, M_WIDTH), c1, n1, m1.reshape(Bd, M_HEADS)


def _merge_body(meta, att_ref, hm_ref, wa_ref, wm_ref, ga_ref, gb_ref, o_ref, *, precise):
    ya = _matmul_tile(att_ref, wa_ref, precise)
    ym = _matmul_tile(hm_ref, wm_ref, precise)
    out = _sigmoid(ga_ref[...].astype(F32)) * ya + _sigmoid(gb_ref[...].astype(F32)) * ym
    o_ref[...] = out.astype(o_ref.dtype)


def merge(meta, att, hm, w_att_out, w_mlstm_out, tail, out_dtype, precise, tn=512):
    T = att.shape[0]
    tm = _m_tile(T)
    ga_off = M_WIDTH // tn
    gb_off = (M_WIDTH + D_MODEL) // tn
    return _pcall(
        functools.partial(_merge_body, precise=precise), (T // tm, D_MODEL // tn),
        [pl.BlockSpec((tm, ATT_OUT), lambda i, j, m: (i, 0)),
         pl.BlockSpec((tm, M_WIDTH), lambda i, j, m: (i, 0)),
         pl.BlockSpec((None, ATT_OUT, tn), lambda i, j, m: (m[0], 0, j)),
         pl.BlockSpec((None, M_WIDTH, tn), lambda i, j, m: (m[0], 0, j)),
         pl.BlockSpec((tm, tn), lambda i, j, m: (i, ga_off + j)),
         pl.BlockSpec((tm, tn), lambda i, j, m: (i, gb_off + j))],
        pl.BlockSpec((tm, tn), lambda i, j, m: (i, j)),
        jax.ShapeDtypeStruct((T, D_MODEL), out_dtype),
        sem=("parallel", "arbitrary"), name="merge")(meta, att, hm, w_att_out, w_mlstm_out, tail, tail)


def _top2_sum(a, b, c, d):
    hi1, lo1 = jnp.maximum(a, b), jnp.minimum(a, b)
    hi2, lo2 = jnp.maximum(c, d), jnp.minimum(c, d)
    return jnp.maximum(hi1, hi2) + jnp.maximum(jnp.minimum(hi1, hi2), jnp.maximum(lo1, lo2))


def _route_rows(s, sb):
    rows = [sb[e:e + 1, :] for e in range(N_EXPERTS)]
    urows = [s[e:e + 1, :] for e in range(N_EXPERTS)]
    gs = [_top2_sum(*rows[g * EPG:(g + 1) * EPG]) for g in range(N_EGROUPS)]
    best = gs[0]
    gsel = jnp.zeros(best.shape, I32)
    for g in range(1, N_EGROUPS):
        upd = gs[g] > best
        gsel = jnp.where(upd, g, gsel)
        best = jnp.where(upd, gs[g], best)
    vals, uvals = [], []
    for i in range(EPG):
        v, u = rows[i], urows[i]
        for g in range(1, N_EGROUPS):
            v = jnp.where(gsel == g, rows[g * EPG + i], v)
            u = jnp.where(gsel == g, urows[g * EPG + i], u)
        vals.append(v)
        uvals.append(u)
    b1, i1, u1 = vals[0], jnp.zeros(best.shape, I32), uvals[0]
    for i in range(1, EPG):
        upd = vals[i] > b1
        i1 = jnp.where(upd, i, i1)
        u1 = jnp.where(upd, uvals[i], u1)
        b1 = jnp.where(upd, vals[i], b1)
    b2 = jnp.full(best.shape, -jnp.inf, F32)
    i2 = jnp.zeros(best.shape, I32)
    u2 = jnp.zeros(best.shape, F32)
    for i in range(EPG):
        upd = (i1 != i) & (vals[i] > b2)
        i2 = jnp.where(upd, i, i2)
        u2 = jnp.where(upd, uvals[i], u2)
        b2 = jnp.where(upd, vals[i], b2)
    tot = u1 + u2
    eidx = jnp.concatenate([gsel * EPG + i1, gsel * EPG + i2], axis=0)
    wts = jnp.concatenate([u1 / tot, u2 / tot], axis=0)
    return eidx, wts


def _norm_route_body(meta, x_ref, g_ref, sc_ref, sh_ref, wr_ref, br_ref, hf_ref, e_ref, w_ref, *, on_mxu):
    hf = _rms(x_ref[...], g_ref[...]) * (1.0 + sc_ref[...]) + sh_ref[...]
    hf_ref[...] = hf.astype(hf_ref.dtype)
    wr = wr_ref[...]
    if on_mxu:
        h_hi, h_lo = _split(hf)
        w_hi, w_lo = _split(wr)
        logits = _dot_nt(w_hi, h_hi) + (_dot_nt(w_hi, h_lo) + _dot_nt(w_lo, h_hi))
    else:
        logits = jnp.sum(wr * hf, axis=1, keepdims=True)
    s = _sigmoid(logits)
    eidx, wts = _route_rows(s, s + br_ref[...])
    e_ref[...] = eidx
    w_ref[...] = wts


def norm_route(meta, x3, g_all, sc, sh, wr_t, b_router, hf_dtype):
    B, S, D = x3.shape
    ts = _row_tile(S)
    hf, eidx, wts = _pcall(
        functools.partial(_norm_route_body, on_mxu=ts >= LANES), (B, S // ts),
        [pl.BlockSpec((None, ts, D), lambda b, i, m: (b, i, 0)),
         pl.BlockSpec((None, 1, D), lambda b, i, m: (m[0], 0, 0)),
         pl.BlockSpec((None, 1, D), lambda b, i, m: (b, 0, 0)),
         pl.BlockSpec((None, 1, D), lambda b, i, m: (b, 0, 0)),
         pl.BlockSpec((N_EXPERTS, D), lambda b, i, m: (0, 0)),
         pl.BlockSpec((N_EXPERTS, 1), lambda b, i, m: (0, 0))],
        [pl.BlockSpec((None, ts, D), lambda b, i, m: (b, i, 0)),
         pl.BlockSpec((None, TOP_K, ts), lambda b, i, m: (b, 0, i)),
         pl.BlockSpec((None, TOP_K, ts), lambda b, i, m: (b, 0, i))],
        [jax.ShapeDtypeStruct((B, S, D), hf_dtype),
         jax.ShapeDtypeStruct((B, TOP_K, S), I32),
         jax.ShapeDtypeStruct((B, TOP_K, S), F32)],
        sem=("parallel", "parallel"), name="norm_route")(
            meta, x3, g_all, sc, sh, wr_t, b_router.reshape(N_EXPERTS, 1))
    T = B * S
    eidx = jnp.swapaxes(eidx, 1, 2).reshape(T, TOP_K)
    wts = jnp.swapaxes(wts, 1, 2).reshape(T, TOP_K)
    return hf.reshape(T, D), eidx, wts


def _expert_body(meta, x_ref, rw_ref, wg_ref, wu_ref, wd_ref, o_ref, *scratch, precise):
    blk = pl.program_id(0)
    n_used = meta[1]
    e = meta[2 + blk]
    e_prev = meta[2 + jnp.maximum(blk - 1, 0)]
    refs = (wg_ref, wu_ref, wd_ref)

    @pl.when((blk == 0) | (e != e_prev))
    def _():
        for i, r in enumerate(refs):
            if precise:
                hi, lo = _split(r[...])
                scratch[2 * i][...] = hi
                scratch[2 * i + 1][...] = lo
            else:
                scratch[i][...] = r[...].astype(BF16)

    @pl.when(blk < n_used)
    def _():
        if precise:
            x_hi, x_lo = _split(x_ref[...])
            h1 = _dot3(x_hi, x_lo, scratch[0][...], scratch[1][...])
            h2 = _dot3(x_hi, x_lo, scratch[2][...], scratch[3][...])
            a_hi, a_lo = _split(h1 * _sigmoid(h1) * h2)
            y = _dot3(a_hi, a_lo, scratch[4][...], scratch[5][...])
        else:
            x = x_ref[...]
            h1 = _dot(x, scratch[0][...])
            h2 = _dot(x, scratch[1][...])
            y = _dot((h1 * _sigmoid(h1) * h2).astype(BF16), scratch[2][...])
        o_ref[...] = y * rw_ref[...]

    @pl.when(blk >= n_used)
    def _():
        o_ref[...] = jnp.zeros_like(o_ref)


def _expert_gather_body(meta, tokn_ref, tok0_ref, dstp_ref, wg_ref, wu_ref, wd_ref, hf_hbm, y_hbm,
                        wg_s, wu_s, wd_s, xbuf, ybuf, sem_in, sem_out, *, bm):
    blk = pl.program_id(0)
    n_used = meta[1]
    e = meta[2 + blk]
    e_prev = meta[2 + jnp.maximum(blk - 1, 0)]
    cur = lax.rem(blk, 2)
    oth = 1 - cur

    def gather(tok_ref, i, slot):
        return pltpu.make_async_copy(hf_hbm.at[pl.ds(tok_ref[0, i], 1), :], xbuf.at[slot, pl.ds(i, 1), :],
                                     sem_in.at[slot])

    def scatter(i, slot):
        return pltpu.make_async_copy(ybuf.at[slot, pl.ds(i, 1), :], y_hbm.at[pl.ds(dstp_ref[0, i], 1), :],
                                     sem_out.at[slot])

    @pl.when((blk == 0) | (e != e_prev))
    def _():
        wg_s[...] = wg_ref[...].astype(BF16)
        wu_s[...] = wu_ref[...].astype(BF16)
        wd_s[...] = wd_ref[...].astype(BF16)

    @pl.when(blk == 0)
    def _():
        ybuf[1] = jnp.zeros((bm, ybuf.shape[2]), F32)
        for i in range(bm):
            gather(tok0_ref, i, 0).start()

    @pl.when(blk <= n_used)
    def _():
        for i in range(bm):
            gather(tokn_ref, i, cur).wait()

    @pl.when((blk >= 1) & (blk <= n_used))
    def _():
        for i in range(bm):
            scatter(i, cur).wait()

    @pl.when(blk < n_used)
    def _():
        x = xbuf[cur].astype(BF16)
        h1 = _dot(x, wg_s[...])
        for i in range(bm):
            gather(tokn_ref, i, oth).start()
        h2 = _dot(x, wu_s[...])
        for i in range(bm):
            scatter(i, oth).start()
        ybuf[cur] = _dot((h1 * _sigmoid(h1) * h2).astype(BF16), wd_s[...])

    @pl.when(blk == n_used)
    def _():
        for i in range(bm):
            scatter(i, oth).start()
        for i in range(bm):
            scatter(i, oth).wait()


def moe(layer, hf, eidx, wts, w_gate, w_up, w_down, bm, precise):
    T, D = hf.shape
    A = T * TOP_K
    n_blocks = -(-A // bm) + N_EXPERTS
    rows = n_blocks * bm
    e_flat = eidx.reshape(A)
    onehot = (e_flat[:, None] == jnp.arange(N_EXPERTS, dtype=I32)[None, :]).astype(I32)
    csum = jnp.cumsum(onehot, axis=0)
    counts = csum[-1]
    rank = jnp.sum((csum - onehot) * onehot, axis=1)
    padded = (counts + bm - 1) // bm * bm
    pad_end = jnp.cumsum(padded)
    pad_start = pad_end - padded
    pos = jnp.sum(pad_start[None, :] * onehot, axis=1) + rank
    n_used = pad_end[-1] // bm
    blk_ids = jnp.arange(n_blocks, dtype=I32)
    blk_exp = jnp.minimum(jnp.sum((pad_end[None, :] <= (blk_ids * bm)[:, None]).astype(I32), axis=1), N_EXPERTS - 1)
    last_exp = blk_exp[jnp.maximum(n_used - 1, 0)]
    blk_exp = jnp.where(blk_ids < n_used, blk_exp, last_exp)
    meta = jnp.concatenate([layer.reshape(1), n_used.reshape(1).astype(I32), blk_exp])
    n_scr = 6 if precise else 3
    scr = []
    for shape in ((D, D_EXPERT), (D, D_EXPERT), (D_EXPERT, D)):
        scr += [pltpu.VMEM(shape, BF16)] * (n_scr // 3)
    a_ids = jnp.arange(A, dtype=I32)
    if not precise:
        y_rows = TOP_K * T + bm
        spare = TOP_K * T + jnp.arange(rows, dtype=I32) % bm
        base = jnp.stack([jnp.zeros((rows,), I32), spare], axis=1)
        vals = jnp.stack([a_ids // TOP_K, (a_ids % TOP_K) * T + a_ids // TOP_K], axis=1)
        tab = base.at[pos].set(vals)
        tok_tab = tab[:, 0].reshape(n_blocks, 1, bm)
        dst_tab = jnp.concatenate([spare[:bm], tab[:, 1]]).reshape(n_blocks + 1, 1, bm)
        smem = functools.partial(pl.BlockSpec, memory_space=pltpu.SMEM)
        return _pcall(
            functools.partial(_expert_gather_body, bm=bm), (n_blocks,),
            [smem((None, 1, bm), lambda b, m: (jnp.minimum(b + 1, n_blocks - 1), 0, 0)),
             smem((None, 1, bm), lambda b, m: (0, 0, 0)),
             smem((None, 1, bm), lambda b, m: (b, 0, 0)),
             pl.BlockSpec((None, None, D, D_EXPERT), lambda b, m: (m[0], m[2 + b], 0, 0)),
             pl.BlockSpec((None, None, D, D_EXPERT), lambda b, m: (m[0], m[2 + b], 0, 0)),
             pl.BlockSpec((None, None, D_EXPERT, D), lambda b, m: (m[0], m[2 + b], 0, 0)),
             pl.BlockSpec(memory_space=pl.ANY)],
            pl.BlockSpec(memory_space=pl.ANY),
            jax.ShapeDtypeStruct((y_rows, D), F32),
            scratch=scr + [pltpu.VMEM((2, bm, D), F32), pltpu.VMEM((2, bm, D), F32),
                           pltpu.SemaphoreType.DMA((2,)), pltpu.SemaphoreType.DMA((2,))],
            sem=("arbitrary",), name="experts_gather")(
                meta, tok_tab, tok_tab, dst_tab, w_gate, w_up, w_down, hf)
    row_tok = jnp.full((rows,), T, I32).at[pos].set(a_ids // TOP_K)
    row_w = jnp.zeros((rows,), F32).at[pos].set(wts.reshape(A))
    xs = jnp.concatenate([hf, jnp.zeros((1, D), hf.dtype)], axis=0)[row_tok]
    ys = _pcall(
        functools.partial(_expert_body, precise=precise), (n_blocks,),
        [pl.BlockSpec((bm, D), lambda b, m: (b, 0)),
         pl.BlockSpec((bm, 1), lambda b, m: (b, 0)),
         pl.BlockSpec((None, None, D, D_EXPERT), lambda b, m: (m[0], m[2 + b], 0, 0)),
         pl.BlockSpec((None, None, D, D_EXPERT), lambda b, m: (m[0], m[2 + b], 0, 0)),
         pl.BlockSpec((None, None, D_EXPERT, D), lambda b, m: (m[0], m[2 + b], 0, 0))],
        pl.BlockSpec((bm, D), lambda b, m: (b, 0)),
        jax.ShapeDtypeStruct((rows, D), F32),
        scratch=scr, sem=("arbitrary",), name="experts")(
            meta, xs, row_w.reshape(rows, 1), w_gate, w_up, w_down)
    return ys[pos.reshape(T, TOP_K).T.reshape(A)].reshape(TOP_K, T, D)


def _moe_residual_body(meta, x_ref, y0_ref, y1_ref, gt_ref, o_ref):
    o_ref[...] = x_ref[...] + gt_ref[...] * (y0_ref[...] + y1_ref[...])


def _moe_residual_w_body(meta, x_ref, y0_ref, y1_ref, gt_ref, w_ref, o_ref):
    w = w_ref[...]
    o_ref[...] = x_ref[...] + gt_ref[...] * (y0_ref[...] * w[:, 0:1] + y1_ref[...] * w[:, 1:2])


def moe_residual(x3, y2, gt, wts=None):
    B, S, D = x3.shape
    ts = _row_tile(S)
    meta = jnp.zeros((1,), I32)
    x_spec = pl.BlockSpec((None, ts, D), lambda b, i, m: (b, i, 0))
    gt_spec = pl.BlockSpec((None, 1, D), lambda b, i, m: (b, 0, 0))
    if y2.ndim == 2:
        per = S // ts
        body = _moe_residual_w_body
        in_specs = [x_spec,
                    pl.BlockSpec((ts, D), lambda b, i, m: (b * per + i, 0)),
                    pl.BlockSpec((ts, D), lambda b, i, m: (B * per + b * per + i, 0)),
                    gt_spec,
                    pl.BlockSpec((None, ts, TOP_K), lambda b, i, m: (b, i, 0))]
        args = (x3, y2, y2, gt, wts.reshape(B, S, TOP_K))
    else:
        y3 = y2.reshape(TOP_K, B, S, D)
        body = _moe_residual_body
        in_specs = [x_spec,
                    pl.BlockSpec((None, None, ts, D), lambda b, i, m: (0, b, i, 0)),
                    pl.BlockSpec((None, None, ts, D), lambda b, i, m: (1, b, i, 0)),
                    gt_spec]
        args = (x3, y3, y3, gt)
    return _pcall(
        body, (B, S // ts), in_specs, x_spec, jax.ShapeDtypeStruct((B, S, D), F32),
        sem=("parallel", "parallel"), name="moe_residual")(meta, *args)


def _rope_tables(pos):
    half = HEAD_DIM // 2
    inv = ROPE_THETA ** (-jnp.arange(half, dtype=F32) / half)
    ang = pos.astype(F32)[:, None] * inv[None, :]
    cos, sin = jnp.cos(ang), jnp.sin(ang)
    return jnp.concatenate([cos, cos], axis=1), jnp.concatenate([-sin, sin], axis=1)


def _layer(layer, x3, mod, params, rope, sample_state):
    (g_mix, g_ffn, w_in, w_gates, w_tail, b_igate, b_fgate, g_mlstm_out, w_att_out, w_mlstm_out,
     w_mix_out, wr_t, b_router, w_exp_gate, w_exp_up, w_exp_down) = params
    B, S, D = x3.shape
    T = B * S
    sample = sample_state is not None
    act = F32 if sample else BF16
    meta = layer.reshape(1)
    sh1, sc1, gt1, sh2, sc2, gt2 = [m.reshape(B, 1, D) for m in jnp.split(mod, 6, axis=-1)]

    wins = []
    if sample:
        h = norm_mod(meta, x3, g_mix, sc1, sh1, act).reshape(T, D)
        qkv = linear(meta, h, w_in, 0, 3 * ATT_WIDTH, F32, True, rot=(rope[0], rope[1], 2 * ATT_WIDTH // 512))
        for g in range(len(ATT_GROUPS)):
            kg = qkv[:, ATT_WIDTH + g * ATT_OUT:ATT_WIDTH + (g + 1) * ATT_OUT]
            vg = qkv[:, 2 * ATT_WIDTH + g * ATT_OUT:2 * ATT_WIDTH + (g + 1) * ATT_OUT]
            wins.append(jnp.concatenate([kg, vg], axis=1).reshape(B, S, 2, ATT_HPG, HEAD_DIM))
    else:
        hs = norm_mod_streams(meta, x3, g_mix, sc1, sh1)
        h = hs[0]
        qkv_g = [linear(meta, hs[g], w_in, g * ATT_OUT, 3 * ATT_OUT, F32, False, rot=(rope[g][0], rope[g][1], 2),
                        col_step=len(ATT_GROUPS)) for g in range(len(ATT_GROUPS))]
        for g, (win, dil) in enumerate(ATT_GROUPS):
            keep = min(win, S)
            L = S // dil
            kv = qkv_g[g].reshape(B, dil, L, 3 * ATT_OUT)[:, :, L - keep // dil:, ATT_OUT:]
            wins.append(jnp.swapaxes(kv, 1, 2).reshape(B, keep, 2, ATT_HPG, HEAD_DIM))
    qkvm = linear(meta, h, w_in, 3 * ATT_WIDTH, 3 * M_WIDTH, act, sample)
    gates = linear(meta, h, w_gates, 0, LANES, F32, sample, tn=LANES)
    tail = linear(meta, h, w_tail, 0, M_WIDTH + 2 * D_MODEL, act, sample)

    if sample:
        caches, st_c, st_n, st_m = sample_state
        att = sample_attention(meta, qkv, caches)
        hm, c1, n1, m1 = mlstm_sample(meta, qkvm, gates, b_igate, b_fgate, tail, g_mlstm_out, st_c, st_n, st_m)
    else:
        outs, lses = zip(*[band_attention(qkv_g[g], B, S, g) for g in range(len(ATT_GROUPS))])
        att = attn_merge(outs, lses)
        hm, c1, n1, m1 = mlstm_prompt(meta, qkvm, gates, b_igate, b_fgate, tail, g_mlstm_out, B, S)

    merged = merge(meta, att, hm, w_att_out, w_mlstm_out, tail, act, sample)
    x2 = linear_residual(meta, merged, w_mix_out, x3.reshape(T, D), gt1, S, sample)
    x3 = x2.reshape(B, S, D)

    hf, eidx, wts = norm_route(meta, x3, g_ffn, sc2, sh2, wr_t, b_router, F32)
    bm = MOE_BLOCK_SAMPLE if sample else MOE_BLOCK_PROMPT
    y2 = moe(layer, hf, eidx, wts, w_exp_gate, w_exp_up, w_exp_down, bm, sample)
    x3 = moe_residual(x3, y2, gt2, None if sample else wts)
    return x3, (wins[0], wins[1], wins[2], c1, n1, m1)


def kernel(x_prompt, x_sample, cache_win_w128, cache_win_w512, cache_win_w2048, state_mlstm_c, state_mlstm_n,
           state_mlstm_m, c_prompt, c_sample, w_ada, b_ada, g_mix, g_ffn, w_in, b_igate, b_fgate, g_mlstm_out,
           w_att_out, w_mlstm_out, w_mix_out, w_router, b_router, w_exp_gate, w_exp_up, w_exp_down, g_final):
    B, S, D = x_prompt.shape
    Bd, Sd, _ = x_sample.shape
    assert Sd == 1 and D == D_MODEL and w_in.shape[2] == D_IN

    n_c = B + Bd
    c_rows = -(-n_c // SUBLANES) * SUBLANES
    c_all = jnp.pad(jnp.concatenate([c_prompt, c_sample], axis=0), ((0, c_rows - n_c), (0, 0)))
    mod = ada_modulation(c_all, w_ada, b_ada)

    w_gates = jnp.pad(w_in[:, :, GATE_OFF:TAIL_OFF], ((0, 0), (0, 0), (0, LANES - 2 * M_HEADS)))
    w_tail = w_in[:, :, TAIL_OFF:]
    params = (g_mix.reshape(DEPTH, 1, D), g_ffn.reshape(DEPTH, 1, D), w_in, w_gates, w_tail, b_igate, b_fgate,
              g_mlstm_out, w_att_out, w_mlstm_out, w_mix_out, w_router.T, b_router,
              w_exp_gate, w_exp_up, w_exp_down)
    rope_p = [_rope_tables((jnp.arange(S // dil, dtype=I32)[None, :] * dil
                            + jnp.arange(dil, dtype=I32)[:, None]).reshape(S)) for _, dil in ATT_GROUPS]
    rope_s = _rope_tables(jnp.full((Bd,), PAST_LEN, I32))
    sample_state = ((cache_win_w128, cache_win_w512, cache_win_w2048), state_mlstm_c, state_mlstm_n, state_mlstm_m)

    def body(carry, layer):
        xp, xs = carry
        mod_l = lax.dynamic_index_in_dim(mod, layer, 0, keepdims=False)
        xp, out_p = _layer(layer, xp, mod_l[:B], params, rope_p, None)
        xs, out_s = _layer(layer, xs, mod_l[B:n_c], params, rope_s, sample_state)
        return (xp, xs), (out_p, out_s)

    (xp, xs), (out_p, out_s) = lax.scan(body, (x_prompt, x_sample), jnp.arange(DEPTH, dtype=I32))
    y_prompt = norm_final(xp, g_final)
    y_sample = norm_final(xs, g_final)
    return (y_prompt, y_sample) + tuple(out_p) + tuple(out_s)
```

```python
import functools

import numpy as np
import jax
import jax.numpy as jnp
from jax import lax
from jax.experimental import pallas as pl
from jax.experimental.pallas import tpu as pltpu

F32 = jnp.float32
BF16 = jnp.bfloat16
I32 = jnp.int32

D_MODEL = 2048
DEPTH = 4
PAST_LEN = 16384
HEAD_DIM = 128
ATT_GROUPS = ((128, 1), (512, 4), (2048, 16))
ATT_HPG = 4
ATT_WIDTH = len(ATT_GROUPS) * ATT_HPG * HEAD_DIM
ATT_OUT = ATT_HPG * HEAD_DIM
ATT_BLOCK = 128
ROPE_THETA = 10000.0
M_HEADS = 4
M_WIDTH = D_MODEL // 2
M_HD = M_WIDTH // M_HEADS
N_EXPERTS = 16
N_EGROUPS = 4
EPG = N_EXPERTS // N_EGROUPS
TOP_K = 2
D_EXPERT = D_MODEL // 4
EPS = 1e-6
NEG_INF = -1e30
D_IN = 3 * ATT_WIDTH + 3 * M_WIDTH + 2 * M_HEADS + M_WIDTH + 2 * D_MODEL
GATE_OFF = 3 * ATT_WIDTH + 3 * M_WIDTH
TAIL_OFF = GATE_OFF + 2 * M_HEADS

V7X_VMEM_BYTES = 64 * 1024 * 1024
LANES = 128
SUBLANES = 8
VMEM_LIMIT = 56 * 1024 * 1024

MLSTM_CHUNK = 256
MOE_BLOCK_PROMPT = 256
MOE_BLOCK_SAMPLE = 8


def _pcall(body, grid, in_specs, out_specs, out_shape, scratch=(), sem=None, name=None):
    return pl.pallas_call(
        body,
        grid_spec=pltpu.PrefetchScalarGridSpec(
            num_scalar_prefetch=1, grid=grid, in_specs=in_specs, out_specs=out_specs,
            scratch_shapes=list(scratch)),
        out_shape=out_shape,
        compiler_params=pltpu.CompilerParams(dimension_semantics=sem, vmem_limit_bytes=VMEM_LIMIT),
        name=name)


def _split(x):
    hi = x.astype(BF16)
    lo = (x - hi.astype(F32)).astype(BF16)
    return hi, lo


def _dot(a, b):
    return jnp.dot(a, b, preferred_element_type=F32)


def _dot_nt(a, b):
    return lax.dot_general(a, b, (((1,), (1,)), ((), ())), preferred_element_type=F32)


def _dot_tn(a, b):
    return lax.dot_general(a, b, (((0,), (0,)), ((), ())), preferred_element_type=F32)


def _dot3(a_hi, a_lo, w_hi, w_lo):
    return _dot(a_hi, w_hi) + (_dot(a_hi, w_lo) + _dot(a_lo, w_hi))


def _sigmoid(x):
    return 1.0 / (1.0 + jnp.exp(-x))


def _log_sigmoid(x):
    return jnp.minimum(x, 0.0) - jnp.log1p(jnp.exp(-jnp.abs(x)))


def _ada_body(meta, c_ref, w_ref, b_ref, o_ref):
    c = c_ref[...]
    a_hi, a_lo = _split(c * _sigmoid(c))
    w_hi, w_lo = _split(w_ref[...])
    o_ref[...] = _dot3(a_hi, a_lo, w_hi, w_lo) + b_ref[...]


def ada_modulation(c_all, w_ada, b_ada):
    rows = c_all.shape[0]
    n_out = w_ada.shape[2]
    tn = 1024
    meta = jnp.zeros((1,), I32)
    return _pcall(
        _ada_body, (DEPTH, n_out // tn),
        [pl.BlockSpec((rows, D_MODEL), lambda l, j, m: (0, 0)),
         pl.BlockSpec((None, D_MODEL, tn), lambda l, j, m: (l, 0, j)),
         pl.BlockSpec((None, 1, tn), lambda l, j, m: (l, 0, j))],
        pl.BlockSpec((None, rows, tn), lambda l, j, m: (l, 0, j)),
        jax.ShapeDtypeStruct((DEPTH, rows, n_out), F32),
        sem=("parallel", "parallel"), name="ada_mod")(meta, c_all, w_ada, b_ada.reshape(DEPTH, 1, n_out))


def _rms(x, g):
    return x * lax.rsqrt(jnp.mean(x * x, axis=-1, keepdims=True) + EPS) * g


def _norm_mod_body(meta, x_ref, g_ref, sc_ref, sh_ref, o_ref):
    y = _rms(x_ref[...], g_ref[...])
    o_ref[...] = (y * (1.0 + sc_ref[...]) + sh_ref[...]).astype(o_ref.dtype)


def _norm_plain_body(meta, x_ref, g_ref, o_ref):
    o_ref[...] = _rms(x_ref[...], g_ref[...]).astype(o_ref.dtype)


def _row_tile(S):
    return min(S, 512)


def norm_mod(meta, x3, g_all, sc, sh, out_dtype):
    B, S, D = x3.shape
    ts = _row_tile(S)
    return _pcall(
        _norm_mod_body, (B, S // ts),
        [pl.BlockSpec((None, ts, D), lambda b, i, m: (b, i, 0)),
         pl.BlockSpec((None, 1, D), lambda b, i, m: (m[0], 0, 0)),
         pl.BlockSpec((None, 1, D), lambda b, i, m: (b, 0, 0)),
         pl.BlockSpec((None, 1, D), lambda b, i, m: (b, 0, 0))],
        pl.BlockSpec((None, ts, D), lambda b, i, m: (b, i, 0)),
        jax.ShapeDtypeStruct((B, S, D), out_dtype),
        sem=("parallel", "parallel"), name="norm_mod")(meta, x3, g_all, sc, sh)


def _norm_streams_body(meta, x_ref, g_ref, sc_ref, sh_ref, o0_ref, o1_ref, o2_ref, y_s, *, ts):
    y = _rms(x_ref[...], g_ref[...]) * (1.0 + sc_ref[...]) + sh_ref[...]
    o0_ref[...] = y.astype(o0_ref.dtype)
    n_slabs = y.shape[1] // LANES
    for c in range(n_slabs):
        y_s[c] = y[:, c * LANES:(c + 1) * LANES]
    for o_ref, (_, dil) in ((o1_ref, ATT_GROUPS[1]), (o2_ref, ATT_GROUPS[2])):
        for r in range(dil):
            for c in range(n_slabs):
                o_ref[r, :, c * LANES:(c + 1) * LANES] = (
                    y_s[c, pl.ds(r, ts // dil, stride=dil), :].astype(o_ref.dtype))


def norm_mod_streams(meta, x3, g_all, sc, sh):
    B, S, D = x3.shape
    ts = _row_tile(S)
    d1, d2 = ATT_GROUPS[1][1], ATT_GROUPS[2][1]
    outs = _pcall(
        functools.partial(_norm_streams_body, ts=ts), (B, S // ts),
        [pl.BlockSpec((None, ts, D), lambda b, i, m: (b, i, 0)),
         pl.BlockSpec((None, 1, D), lambda b, i, m: (m[0], 0, 0)),
         pl.BlockSpec((None, 1, D), lambda b, i, m: (b, 0, 0)),
         pl.BlockSpec((None, 1, D), lambda b, i, m: (b, 0, 0))],
        [pl.BlockSpec((None, ts, D), lambda b, i, m: (b, i, 0)),
         pl.BlockSpec((None, d1, ts // d1, D), lambda b, i, m: (b, 0, i, 0)),
         pl.BlockSpec((None, d2, ts // d2, D), lambda b, i, m: (b, 0, i, 0))],
        [jax.ShapeDtypeStruct((B, S, D), BF16),
         jax.ShapeDtypeStruct((B, d1, S // d1, D), BF16),
         jax.ShapeDtypeStruct((B, d2, S // d2, D), BF16)],
        scratch=[pltpu.VMEM((D // LANES, ts, LANES), F32)],
        sem=("parallel", "parallel"), name="norm_mod_streams")(meta, x3, g_all, sc, sh)
    return [o.reshape(B * S, D) for o in outs]


def norm_final(x3, g):
    B, S, D = x3.shape
    ts = _row_tile(S)
    meta = jnp.zeros((1,), I32)
    return _pcall(
        _norm_plain_body, (B, S // ts),
        [pl.BlockSpec((None, ts, D), lambda b, i, m: (b, i, 0)),
         pl.BlockSpec((1, D), lambda b, i, m: (0, 0))],
        pl.BlockSpec((None, ts, D), lambda b, i, m: (b, i, 0)),
        jax.ShapeDtypeStruct((B, S, D), F32),
        sem=("parallel", "parallel"), name="norm_final")(meta, x3, g.reshape(1, D))


def _matmul_tile(a_ref, w_ref, precise):
    w = w_ref[...]
    if precise:
        a_hi, a_lo = _split(a_ref[...])
        w_hi, w_lo = _split(w)
        return _dot3(a_hi, a_lo, w_hi, w_lo)
    return _dot(a_ref[...], w.astype(BF16))


def _linear_body(meta, a_ref, w_ref, o_ref, *, precise):
    o_ref[...] = _matmul_tile(a_ref, w_ref, precise).astype(o_ref.dtype)


def _linear_rot_body(meta, a_ref, w_ref, cos_ref, sin_ref, o_ref, *, precise, n_rot, tn):
    acc = _matmul_tile(a_ref, w_ref, precise)
    j = pl.program_id(1)

    @pl.when(j < n_rot)
    def _():
        c = cos_ref[...]
        s = sin_ref[...]
        for h in range(tn // HEAD_DIM):
            hs = slice(h * HEAD_DIM, (h + 1) * HEAD_DIM)
            x = acc[:, hs]
            o_ref[:, hs] = (x * c + pltpu.roll(x, HEAD_DIM // 2, 1) * s).astype(o_ref.dtype)

    @pl.when(j >= n_rot)
    def _():
        o_ref[...] = acc.astype(o_ref.dtype)


def _linear_res_body(meta, a_ref, w_ref, x_ref, gt_ref, o_ref, *, precise):
    o_ref[...] = x_ref[...] + gt_ref[...] * _matmul_tile(a_ref, w_ref, precise)


def _m_tile(T):
    return min(T, 2048)


def linear(meta, a, w, col_off, n_out, out_dtype, precise, tn=512, rot=None, col_step=1):
    T, K = a.shape
    tm = _m_tile(T)
    off = col_off // tn
    in_specs = [pl.BlockSpec((tm, K), lambda i, j, m: (i, 0)),
                pl.BlockSpec((None, K, tn), lambda i, j, m: (m[0], 0, off + j * col_step))]
    args = [a, w]
    if rot is None:
        body = functools.partial(_linear_body, precise=precise)
    else:
        cos_t, sin_t, n_rot = rot
        nper = cos_t.shape[0] // tm
        in_specs += [pl.BlockSpec((tm, HEAD_DIM), lambda i, j, m: (i % nper, 0)),
                     pl.BlockSpec((tm, HEAD_DIM), lambda i, j, m: (i % nper, 0))]
        args += [cos_t, sin_t]
        body = functools.partial(_linear_rot_body, precise=precise, n_rot=n_rot, tn=tn)
    return _pcall(
        body, (T // tm, n_out // tn), in_specs,
        pl.BlockSpec((tm, tn), lambda i, j, m: (i, j)),
        jax.ShapeDtypeStruct((T, n_out), out_dtype),
        sem=("parallel", "arbitrary"), name="linear")(meta, *args)


def linear_residual(meta, a, w, x2, gt, rows_per_batch, precise, tn=512):
    T, K = a.shape
    D = x2.shape[1]
    tm = _m_tile(T)
    if rows_per_batch == 1:
        gt_arr = gt.reshape(T, D)
        gt_spec = pl.BlockSpec((tm, tn), lambda i, j, m: (i, j))
    else:
        per = rows_per_batch // tm
        gt_arr = gt
        gt_spec = pl.BlockSpec((None, 1, tn), lambda i, j, m: (i // per, 0, j))
    return _pcall(
        functools.partial(_linear_res_body, precise=precise), (T // tm, D // tn),
        [pl.BlockSpec((tm, K), lambda i, j, m: (i, 0)),
         pl.BlockSpec((None, K, tn), lambda i, j, m: (m[0], 0, j)),
         pl.BlockSpec((tm, tn), lambda i, j, m: (i, j)),
         gt_spec],
        pl.BlockSpec((tm, tn), lambda i, j, m: (i, j)),
        jax.ShapeDtypeStruct((T, D), F32),
        sem=("parallel", "arbitrary"), name="linear_residual")(meta, a, w, x2, gt_arr)


def _attn_body(meta, q_ref, k_ref, v_ref, o_ref, lse_ref, *, nb):
    scale = HEAD_DIM ** -0.5
    qi = lax.broadcasted_iota(I32, (ATT_BLOCK, ATT_BLOCK), 0)
    kj = lax.broadcasted_iota(I32, (ATT_BLOCK, ATT_BLOCK), 1)
    m_cur = kj <= qi
    m_prev = qi <= kj

    def blk(n, carry):
        qs = pl.multiple_of(n * ATT_BLOCK, ATT_BLOCK)
        ps = pl.multiple_of(jnp.maximum(n - 1, 0) * ATT_BLOCK, ATT_BLOCK)
        prev_cap = jnp.where(n > 0, -NEG_INF, NEG_INF)
        for h in range(ATT_HPG):
            hs = slice(h * HEAD_DIM, (h + 1) * HEAD_DIM)
            q = q_ref[pl.ds(qs, ATT_BLOCK), hs].astype(BF16)
            kc = k_ref[pl.ds(qs, ATT_BLOCK), hs].astype(BF16)
            kp = k_ref[pl.ds(ps, ATT_BLOCK), hs].astype(BF16)
            vc = v_ref[pl.ds(qs, ATT_BLOCK), hs].astype(BF16)
            vp = v_ref[pl.ds(ps, ATT_BLOCK), hs].astype(BF16)
            sc = jnp.where(m_cur, _dot_nt(q, kc) * scale, NEG_INF)
            sp = jnp.minimum(jnp.where(m_prev, _dot_nt(q, kp) * scale, NEG_INF), prev_cap)
            mx = jnp.maximum(jnp.max(sc, axis=1, keepdims=True), jnp.max(sp, axis=1, keepdims=True))
            pc = jnp.exp(sc - mx)
            pp = jnp.exp(sp - mx)
            den = jnp.sum(pc, axis=1, keepdims=True) + jnp.sum(pp, axis=1, keepdims=True)
            o = (_dot(pc.astype(BF16), vc) + _dot(pp.astype(BF16), vp)) / den
            o_ref[pl.ds(qs, ATT_BLOCK), hs] = o.astype(o_ref.dtype)
            lse_ref[pl.ds(qs, ATT_BLOCK), hs] = jnp.broadcast_to(mx + jnp.log(den), (ATT_BLOCK, HEAD_DIM))
        return carry

    lax.fori_loop(0, nb, blk, 0)


def band_attention(qkv_g, B, S, g):
    _, dil = ATT_GROUPS[g]
    L = S // dil
    assert L % ATT_BLOCK == 0
    qv = qkv_g.reshape(B, dil, L, 3 * ATT_OUT)
    meta = jnp.zeros((1,), I32)
    o, lse = _pcall(
        functools.partial(_attn_body, nb=L // ATT_BLOCK), (B, dil),
        [pl.BlockSpec((None, None, L, ATT_OUT), lambda b, r, m: (b, r, 0, 0)),
         pl.BlockSpec((None, None, L, ATT_OUT), lambda b, r, m: (b, r, 0, 1)),
         pl.BlockSpec((None, None, L, ATT_OUT), lambda b, r, m: (b, r, 0, 2))],
        [pl.BlockSpec((None, L, ATT_OUT), lambda b, r, m: (b, 0, r)),
         pl.BlockSpec((None, L, ATT_OUT), lambda b, r, m: (b, 0, r))],
        [jax.ShapeDtypeStruct((B, L, dil * ATT_OUT), BF16),
         jax.ShapeDtypeStruct((B, L, dil * ATT_OUT), F32)],
        sem=("parallel", "parallel"), name="band_attention")(meta, qv, qv, qv)
    return o.reshape(B * S, ATT_OUT), lse.reshape(B * S, ATT_OUT)


def _attn_merge_body(meta, o0, o1, o2, l0, l1, l2, att_ref):
    a0, a1, a2 = l0[...], l1[...], l2[...]
    mx = jnp.maximum(jnp.maximum(a0, a1), a2)
    e0, e1, e2 = jnp.exp(a0 - mx), jnp.exp(a1 - mx), jnp.exp(a2 - mx)
    num = e0 * o0[...].astype(F32) + e1 * o1[...].astype(F32) + e2 * o2[...].astype(F32)
    att_ref[...] = (num / (e0 + e1 + e2)).astype(att_ref.dtype)


def attn_merge(outs, lses):
    T = outs[0].shape[0]
    tm = _m_tile(T)
    spec = pl.BlockSpec((tm, ATT_OUT), lambda i, m: (i, 0))
    meta = jnp.zeros((1,), I32)
    return _pcall(_attn_merge_body, (T // tm,), [spec] * 6, spec,
                  jax.ShapeDtypeStruct((T, ATT_OUT), BF16),
                  sem=("parallel",), name="attn_merge")(meta, *outs, *lses)


def _attn_sample_body(meta, q_ref, c0_ref, c1_ref, c2_ref, att_ref):
    scale = HEAD_DIM ** -0.5
    row = q_ref[...]
    caches = (c0_ref, c1_ref, c2_ref)
    for h in range(ATT_HPG):
        outs, lses = [], []
        for g in range(len(ATT_GROUPS)):
            col = (g * ATT_HPG + h) * HEAD_DIM
            q = row[:, col:col + HEAD_DIM]
            k_new = row[:, ATT_WIDTH + col:ATT_WIDTH + col + HEAD_DIM]
            v_new = row[:, 2 * ATT_WIDTH + col:2 * ATT_WIDTH + col + HEAD_DIM]
            kc = caches[g][:, h * HEAD_DIM:(h + 1) * HEAD_DIM]
            vc = caches[g][:, ATT_OUT + h * HEAD_DIM:ATT_OUT + (h + 1) * HEAD_DIM]
            s = jnp.sum(kc * q, axis=1, keepdims=True) * scale
            s0 = jnp.sum(k_new * q, axis=1, keepdims=True) * scale
            mx = jnp.maximum(jnp.max(s, axis=0, keepdims=True), s0)
            p = jnp.exp(s - mx)
            p0 = jnp.exp(s0 - mx)
            den = jnp.sum(p, axis=0, keepdims=True) + p0
            outs.append((jnp.sum(vc * p, axis=0, keepdims=True) + p0 * v_new) / den)
            lses.append(mx + jnp.log(den))
        mxl = jnp.maximum(jnp.maximum(lses[0], lses[1]), lses[2])
        es = [jnp.exp(l - mxl) for l in lses]
        num = es[0] * outs[0] + es[1] * outs[1] + es[2] * outs[2]
        att_ref[:, h * HEAD_DIM:(h + 1) * HEAD_DIM] = num / (es[0] + es[1] + es[2])


def sample_attention(meta, qkv, caches):
    Bd = qkv.shape[0]
    wcols = 3 * ATT_WIDTH
    in_specs = [pl.BlockSpec((None, 1, wcols), lambda b, m: (b, 0, 0))]
    args = [qkv.reshape(Bd, 1, wcols)]
    for g in range(len(ATT_GROUPS)):
        args.append(caches[g])
        in_specs.append(pl.BlockSpec((None, None, ATT_BLOCK, 2 * ATT_OUT), lambda b, m: (m[0], b, 0, 0)))
    att = _pcall(
        _attn_sample_body, (Bd,), in_specs,
        pl.BlockSpec((None, 1, ATT_OUT), lambda b, m: (b, 0, 0)),
        jax.ShapeDtypeStruct((Bd, 1, ATT_OUT), F32),
        sem=("parallel",), name="sample_attention")(meta, *args)
    return att.reshape(Bd, ATT_OUT)


def _mlstm_body(meta, q_ref, k_ref, v_ref, gc_ref, gr_ref, bc_ref, br_ref, og_ref, g_ref,
                hm_ref, caug_ref, m_ref, c_s, m_s, *, Lc, nc):
    h = pl.program_id(1)
    c = pl.program_id(2)

    @pl.when(c == 0)
    def _():
        c_s[...] = jnp.zeros_like(c_s)
        m_s[...] = jnp.zeros_like(m_s)

    gcol = gc_ref[...] + bc_ref[...]
    lane = lax.broadcasted_iota(I32, gcol.shape, 1)
    i_col = jnp.sum(jnp.where(lane == h, gcol, 0.0), axis=1, keepdims=True)
    f_col = jnp.sum(jnp.where(lane == h + M_HEADS, gcol, 0.0), axis=1, keepdims=True)
    grow = gr_ref[...] + br_ref[...]
    sub = lax.broadcasted_iota(I32, grow.shape, 0)
    i_row = jnp.sum(jnp.where(sub == h, grow, 0.0), axis=0, keepdims=True)
    f_row = jnp.sum(jnp.where(sub == h + M_HEADS, grow, 0.0), axis=0, keepdims=True)
    lf_col = _log_sigmoid(f_col)
    lf_row = _log_sigmoid(f_row)

    t_i = lax.broadcasted_iota(I32, (Lc, Lc), 0)
    s_i = lax.broadcasted_iota(I32, (Lc, Lc), 1)
    causal = s_i <= t_i
    b_col = jnp.sum(jnp.where(causal, lf_row, 0.0), axis=1, keepdims=True)
    b_row = jnp.sum(jnp.where(t_i <= s_i, lf_col, 0.0), axis=0, keepdims=True)
    a_row = i_row - b_row
    cmax_col = jnp.max(jnp.where(causal, a_row, -jnp.inf), axis=1, keepdims=True)
    m_prev = m_s[0:1, 0:1]
    m_t = b_col + jnp.maximum(m_prev, cmax_col)
    inter = jnp.exp(m_prev + b_col - m_t)
    dmat = jnp.exp(jnp.where(causal, a_row + (b_col - m_t), NEG_INF))

    q = q_ref[...]
    k = k_ref[...]
    v = v_ref[...]
    kscale = M_HD ** -0.5
    one_col = jnp.where(lax.broadcasted_iota(I32, (Lc, LANES), 1) == 0, 1.0, 0.0).astype(BF16)
    v_aug = jnp.concatenate([v, one_col], axis=1)
    w = _dot_nt(q, k) * kscale * dmat
    caug = c_s[...]
    num = inter * _dot(q, caug.astype(BF16)) + _dot(w.astype(BF16), v_aug)
    den = num[:, M_HD:M_HD + 1]
    hh = num[:, :M_HD] / jnp.maximum(jnp.abs(den), jnp.exp(-m_t))

    b_last = jnp.sum(lf_row, axis=1, keepdims=True)
    m_last = b_last + jnp.maximum(m_prev, jnp.max(a_row, axis=1, keepdims=True))
    decay = jnp.exp(m_prev + b_last - m_last)
    wk_col = jnp.exp(i_col + b_last - b_col - m_last)
    kw = (k.astype(F32) * (wk_col * kscale)).astype(BF16)
    c_new = decay * caug + _dot_tn(kw, v_aug)
    c_s[...] = c_new
    m_s[...] = jnp.broadcast_to(m_last, m_s.shape)

    y = _rms(hh, g_ref[...]) * _sigmoid(og_ref[...].astype(F32))
    hm_ref[...] = y.astype(hm_ref.dtype)

    @pl.when(c == nc - 1)
    def _():
        caug_ref[...] = c_new
        m_ref[...] = jnp.broadcast_to(m_last, m_ref.shape)


def mlstm_prompt(meta, qkvm, gates, b_i, b_f, tail, g_out, B, S):
    Lc = min(MLSTM_CHUNK, S)
    nc = S // Lc
    caw = M_HD + LANES
    q3 = qkvm.reshape(B, S, 3 * M_WIDTH)
    g3 = gates.reshape(B, S, LANES)
    gr = jnp.swapaxes(g3[:, :, :SUBLANES], 1, 2)
    bias = jnp.concatenate([b_i, b_f], axis=1)
    bc = jnp.pad(bias, ((0, 0), (0, LANES - 2 * M_HEADS))).reshape(DEPTH, 1, LANES)
    br = bias.reshape(DEPTH, 2 * M_HEADS, 1)
    t3 = tail.reshape(B, S, tail.shape[1])
    hm, caug, mfin = _pcall(
        functools.partial(_mlstm_body, Lc=Lc, nc=nc), (B, M_HEADS, nc),
        [pl.BlockSpec((None, Lc, M_HD), lambda b, h, c, m: (b, c, h)),
         pl.BlockSpec((None, Lc, M_HD), lambda b, h, c, m: (b, c, M_HEADS + h)),
         pl.BlockSpec((None, Lc, M_HD), lambda b, h, c, m: (b, c, 2 * M_HEADS + h)),
         pl.BlockSpec((None, Lc, LANES), lambda b, h, c, m: (b, c, 0)),
         pl.BlockSpec((None, SUBLANES, Lc), lambda b, h, c, m: (b, 0, c)),
         pl.BlockSpec((None, 1, LANES), lambda b, h, c, m: (m[0], 0, 0)),
         pl.BlockSpec((None, 2 * M_HEADS, 1), lambda b, h, c, m: (m[0], 0, 0)),
         pl.BlockSpec((None, Lc, M_HD), lambda b, h, c, m: (b, c, h)),
         pl.BlockSpec((None, 1, M_HD), lambda b, h, c, m: (m[0], 0, h))],
        [pl.BlockSpec((None, Lc, M_HD), lambda b, h, c, m: (b, c, h)),
         pl.BlockSpec((None, None, M_HD, caw), lambda b, h, c, m: (b, h, 0, 0)),
         pl.BlockSpec((None, None, SUBLANES, LANES), lambda b, h, c, m: (b, h, 0, 0))],
        [jax.ShapeDtypeStruct((B, S, M_WIDTH), BF16),
         jax.ShapeDtypeStruct((B, M_HEADS, M_HD, caw), F32),
         jax.ShapeDtypeStruct((B, M_HEADS, SUBLANES, LANES), F32)],
        scratch=[pltpu.VMEM((M_HD, caw), F32), pltpu.VMEM((SUBLANES, LANES), F32)],
        sem=("parallel", "parallel", "arbitrary"), name="mlstm_prompt")(
            meta, q3, q3, q3, g3, gr, bc, br, t3, g_out.reshape(DEPTH, 1, M_WIDTH))
    return (hm.reshape(B * S, M_WIDTH), caug[..., :M_HD], caug[..., M_HD], mfin[:, :, 0, 0])


def _mlstm_sample_body(meta, x_ref, g_ref, bc_ref, og_ref, gout_ref, c0_ref, n0_ref, m0_ref,
                       hm_ref, c1_ref, n1_ref, m1_ref):
    row = x_ref[...]
    gates = g_ref[...] + bc_ref[...]
    eye = lax.broadcasted_iota(I32, (M_HD, M_HD), 0) == lax.broadcasted_iota(I32, (M_HD, M_HD), 1)
    kscale = M_HD ** -0.5
    for h in range(M_HEADS):
        hs = slice(h * M_HD, (h + 1) * M_HD)
        q = row[:, h * M_HD:(h + 1) * M_HD]
        k = row[:, M_WIDTH + h * M_HD:M_WIDTH + (h + 1) * M_HD] * kscale
        v = row[:, 2 * M_WIDTH + h * M_HD:2 * M_WIDTH + (h + 1) * M_HD]
        q_col = jnp.sum(jnp.where(eye, q, 0.0), axis=1, keepdims=True)
        k_col = jnp.sum(jnp.where(eye, k, 0.0), axis=1, keepdims=True)
        ii = gates[:, h:h + 1]
        lf = _log_sigmoid(gates[:, M_HEADS + h:M_HEADS + h + 1])
        C = c0_ref[h]
        n = n0_ref[h:h + 1, :]
        m = m0_ref[:, h:h + 1]
        a = ii - lf
        m_t = lf + jnp.maximum(m, a)
        inter = jnp.exp(m + lf - m_t)
        dm = jnp.exp(a + (lf - m_t))
        w = jnp.sum(q * k, axis=1, keepdims=True) * dm
        num = inter * jnp.sum(C * q_col, axis=0, keepdims=True) + w * v
        den = inter * jnp.sum(q * n, axis=1, keepdims=True) + w
        hh = num / jnp.maximum(jnp.abs(den), jnp.exp(-m_t))
        wk = jnp.exp(ii + lf - lf - m_t)
        c1_ref[h] = inter * C + (wk * k_col) * v
        n1_ref[h:h + 1, :] = inter * n + wk * k
        m1_ref[:, h:h + 1] = m_t
        y = _rms(hh, gout_ref[:, hs]) * _sigmoid(og_ref[:, hs])
        hm_ref[:, hs] = y


def mlstm_sample(meta, qkvm, gates, b_i, b_f, tail, g_out, st_c, st_n, st_m):
    Bd = qkvm.shape[0]
    bias = jnp.concatenate([b_i, b_f], axis=1)
    bc = jnp.pad(bias, ((0, 0), (0, LANES - 2 * M_HEADS))).reshape(DEPTH, 1, LANES)
    tw = tail.shape[1]
    hm, c1, n1, m1 = _pcall(
        _mlstm_sample_body, (Bd,),
        [pl.BlockSpec((None, 1, 3 * M_WIDTH), lambda b, m: (b, 0, 0)),
         pl.BlockSpec((None, 1, LANES), lambda b, m: (b, 0, 0)),
         pl.BlockSpec((None, 1, LANES), lambda b, m: (m[0], 0, 0)),
         pl.BlockSpec((None, 1, M_WIDTH), lambda b, m: (b, 0, 0)),
         pl.BlockSpec((None, 1, M_WIDTH), lambda b, m: (m[0], 0, 0)),
         pl.BlockSpec((None, None, M_HEADS, M_HD, M_HD), lambda b, m: (m[0], b, 0, 0, 0)),
         pl.BlockSpec((None, None, M_HEADS, M_HD), lambda b, m: (m[0], b, 0, 0)),
         pl.BlockSpec((None, None, 1, M_HEADS), lambda b, m: (m[0], b, 0, 0))],
        [pl.BlockSpec((None, 1, M_WIDTH), lambda b, m: (b, 0, 0)),
         pl.BlockSpec((None, M_HEADS, M_HD, M_HD), lambda b, m: (b, 0, 0, 0)),
         pl.BlockSpec((None, M_HEADS, M_HD), lambda b, m: (b, 0, 0)),
         pl.BlockSpec((None, 1, M_HEADS), lambda b, m: (b, 0, 0))],
        [jax.ShapeDtypeStruct((Bd, 1, M_WIDTH), F32),
         jax.ShapeDtypeStruct((Bd, M_HEADS, M_HD, M_HD), F32),
         jax.ShapeDtypeStruct((Bd, M_HEADS, M_HD), F32),
         jax.ShapeDtypeStruct((Bd, 1, M_HEADS), F32)],
        sem=("parallel",), name="mlstm_sample")(
            meta, qkvm.reshape(Bd, 1, 3 * M_WIDTH), gates.reshape(Bd, 1, LANES), bc,
            tail.reshape(Bd, 1, tw), g_out.reshape(DEPTH, 1, M_WIDTH),
            st_c, st_n, st_m.reshape(DEPTH, Bd, 1, M_HEADS))
    return hm.reshape(Bd, M_WIDTH), c1, n1, m1.reshape(Bd, M_HEADS)


def _merge_body(meta, att_ref, hm_ref, wa_ref, wm_ref, ga_ref, gb_ref, o_ref, *, precise):
    ya = _matmul_tile(att_ref, wa_ref, precise)
    ym = _matmul_tile(hm_ref, wm_ref, precise)
    out = _sigmoid(ga_ref[...].astype(F32)) * ya + _sigmoid(gb_ref[...].astype(F32)) * ym
    o_ref[...] = out.astype(o_ref.dtype)


def merge(meta, att, hm, w_att_out, w_mlstm_out, tail, out_dtype, precise, tn=512):
    T = att.shape[0]
    tm = _m_tile(T)
    ga_off = M_WIDTH // tn
    gb_off = (M_WIDTH + D_MODEL) // tn
    return _pcall(
        functools.partial(_merge_body, precise=precise), (T // tm, D_MODEL // tn),
        [pl.BlockSpec((tm, ATT_OUT), lambda i, j, m: (i, 0)),
         pl.BlockSpec((tm, M_WIDTH), lambda i, j, m: (i, 0)),
         pl.BlockSpec((None, ATT_OUT, tn), lambda i, j, m: (m[0], 0, j)),
         pl.BlockSpec((None, M_WIDTH, tn), lambda i, j, m: (m[0], 0, j)),
         pl.BlockSpec((tm, tn), lambda i, j, m: (i, ga_off + j)),
         pl.BlockSpec((tm, tn), lambda i, j, m: (i, gb_off + j))],
        pl.BlockSpec((tm, tn), lambda i, j, m: (i, j)),
        jax.ShapeDtypeStruct((T, D_MODEL), out_dtype),
        sem=("parallel", "arbitrary"), name="merge")(meta, att, hm, w_att_out, w_mlstm_out, tail, tail)


def _top2_sum(a, b, c, d):
    hi1, lo1 = jnp.maximum(a, b), jnp.minimum(a, b)
    hi2, lo2 = jnp.maximum(c, d), jnp.minimum(c, d)
    return jnp.maximum(hi1, hi2) + jnp.maximum(jnp.minimum(hi1, hi2), jnp.maximum(lo1, lo2))


def _route_rows(s, sb):
    rows = [sb[e:e + 1, :] for e in range(N_EXPERTS)]
    urows = [s[e:e + 1, :] for e in range(N_EXPERTS)]
    gs = [_top2_sum(*rows[g * EPG:(g + 1) * EPG]) for g in range(N_EGROUPS)]
    best = gs[0]
    gsel = jnp.zeros(best.shape, I32)
    for g in range(1, N_EGROUPS):
        upd = gs[g] > best
        gsel = jnp.where(upd, g, gsel)
        best = jnp.where(upd, gs[g], best)
    vals, uvals = [], []
    for i in range(EPG):
        v, u = rows[i], urows[i]
        for g in range(1, N_EGROUPS):
            v = jnp.where(gsel == g, rows[g * EPG + i], v)
            u = jnp.where(gsel == g, urows[g * EPG + i], u)
        vals.append(v)
        uvals.append(u)
    b1, i1, u1 = vals[0], jnp.zeros(best.shape, I32), uvals[0]
    for i in range(1, EPG):
        upd = vals[i] > b1
        i1 = jnp.where(upd, i, i1)
        u1 = jnp.where(upd, uvals[i], u1)
        b1 = jnp.where(upd, vals[i], b1)
    b2 = jnp.full(best.shape, -jnp.inf, F32)
    i2 = jnp.zeros(best.shape, I32)
    u2 = jnp.zeros(best.shape, F32)
    for i in range(EPG):
        upd = (i1 != i) & (vals[i] > b2)
        i2 = jnp.where(upd, i, i2)
        u2 = jnp.where(upd, uvals[i], u2)
        b2 = jnp.where(upd, vals[i], b2)
    tot = u1 + u2
    eidx = jnp.concatenate([gsel * EPG + i1, gsel * EPG + i2], axis=0)
    wts = jnp.concatenate([u1 / tot, u2 / tot], axis=0)
    return eidx, wts


def _norm_route_body(meta, x_ref, g_ref, sc_ref, sh_ref, wr_ref, br_ref, hf_ref, e_ref, w_ref, *, on_mxu):
    hf = _rms(x_ref[...], g_ref[...]) * (1.0 + sc_ref[...]) + sh_ref[...]
    hf_ref[...] = hf.astype(hf_ref.dtype)
    wr = wr_ref[...]
    if on_mxu:
        h_hi, h_lo = _split(hf)
        w_hi, w_lo = _split(wr)
        logits = _dot_nt(w_hi, h_hi) + (_dot_nt(w_hi, h_lo) + _dot_nt(w_lo, h_hi))
    else:
        logits = jnp.sum(wr * hf, axis=1, keepdims=True)
    s = _sigmoid(logits)
    eidx, wts = _route_rows(s, s + br_ref[...])
    e_ref[...] = eidx
    w_ref[...] = wts


def norm_route(meta, x3, g_all, sc, sh, wr_t, b_router, hf_dtype):
    B, S, D = x3.shape
    ts = _row_tile(S)
    hf, eidx, wts = _pcall(
        functools.partial(_norm_route_body, on_mxu=ts >= LANES), (B, S // ts),
        [pl.BlockSpec((None, ts, D), lambda b, i, m: (b, i, 0)),
         pl.BlockSpec((None, 1, D), lambda b, i, m: (m[0], 0, 0)),
         pl.BlockSpec((None, 1, D), lambda b, i, m: (b, 0, 0)),
         pl.BlockSpec((None, 1, D), lambda b, i, m: (b, 0, 0)),
         pl.BlockSpec((N_EXPERTS, D), lambda b, i, m: (0, 0)),
         pl.BlockSpec((N_EXPERTS, 1), lambda b, i, m: (0, 0))],
        [pl.BlockSpec((None, ts, D), lambda b, i, m: (b, i, 0)),
         pl.BlockSpec((None, TOP_K, ts), lambda b, i, m: (b, 0, i)),
         pl.BlockSpec((None, TOP_K, ts), lambda b, i, m: (b, 0, i))],
        [jax.ShapeDtypeStruct((B, S, D), hf_dtype),
         jax.ShapeDtypeStruct((B, TOP_K, S), I32),
         jax.ShapeDtypeStruct((B, TOP_K, S), F32)],
        sem=("parallel", "parallel"), name="norm_route")(
            meta, x3, g_all, sc, sh, wr_t, b_router.reshape(N_EXPERTS, 1))
    T = B * S
    eidx = jnp.swapaxes(eidx, 1, 2).reshape(T, TOP_K)
    wts = jnp.swapaxes(wts, 1, 2).reshape(T, TOP_K)
    return hf.reshape(T, D), eidx, wts


def _expert_body(meta, x_ref, rw_ref, wg_ref, wu_ref, wd_ref, o_ref, *scratch, precise):
    blk = pl.program_id(0)
    n_used = meta[1]
    e = meta[2 + blk]
    e_prev = meta[2 + jnp.maximum(blk - 1, 0)]
    refs = (wg_ref, wu_ref, wd_ref)

    @pl.when((blk == 0) | (e != e_prev))
    def _():
        for i, r in enumerate(refs):
            if precise:
                hi, lo = _split(r[...])
                scratch[2 * i][...] = hi
                scratch[2 * i + 1][...] = lo
            else:
                scratch[i][...] = r[...].astype(BF16)

    @pl.when(blk < n_used)
    def _():
        if precise:
            x_hi, x_lo = _split(x_ref[...])
            h1 = _dot3(x_hi, x_lo, scratch[0][...], scratch[1][...])
            h2 = _dot3(x_hi, x_lo, scratch[2][...], scratch[3][...])
            a_hi, a_lo = _split(h1 * _sigmoid(h1) * h2)
            y = _dot3(a_hi, a_lo, scratch[4][...], scratch[5][...])
        else:
            x = x_ref[...]
            h1 = _dot(x, scratch[0][...])
            h2 = _dot(x, scratch[1][...])
            y = _dot((h1 * _sigmoid(h1) * h2).astype(BF16), scratch[2][...])
        o_ref[...] = y * rw_ref[...]

    @pl.when(blk >= n_used)
    def _():
        o_ref[...] = jnp.zeros_like(o_ref)


def _expert_gather_body(meta, tokn_ref, tok0_ref, dstp_ref, wg_ref, wu_ref, wd_ref, hf_hbm, y_hbm,
                        wg_s, wu_s, wd_s, xbuf, ybuf, sem_in, sem_out, *, bm):
    blk = pl.program_id(0)
    n_used = meta[1]
    e = meta[2 + blk]
    e_prev = meta[2 + jnp.maximum(blk - 1, 0)]
    cur = lax.rem(blk, 2)
    oth = 1 - cur

    def gather(tok_ref, i, slot):
        return pltpu.make_async_copy(hf_hbm.at[pl.ds(tok_ref[0, i], 1), :], xbuf.at[slot, pl.ds(i, 1), :],
                                     sem_in.at[slot])

    def scatter(i, slot):
        return pltpu.make_async_copy(ybuf.at[slot, pl.ds(i, 1), :], y_hbm.at[pl.ds(dstp_ref[0, i], 1), :],
                                     sem_out.at[slot])

    @pl.when((blk == 0) | (e != e_prev))
    def _():
        wg_s[...] = wg_ref[...].astype(BF16)
        wu_s[...] = wu_ref[...].astype(BF16)
        wd_s[...] = wd_ref[...].astype(BF16)

    @pl.when(blk == 0)
    def _():
        ybuf[1] = jnp.zeros((bm, ybuf.shape[2]), F32)
        for i in range(bm):
            gather(tok0_ref, i, 0).start(priority=i % 2)

    @pl.when(blk <= n_used)
    def _():
        for i in range(bm):
            gather(tokn_ref, i, cur).wait()

    @pl.when((blk >= 1) & (blk <= n_used))
    def _():
        for i in range(bm):
            scatter(i, cur).wait()

    @pl.when(blk < n_used)
    def _():
        x = xbuf[cur].astype(BF16)
        h1 = _dot(x, wg_s[...])
        for i in range(bm):
            gather(tokn_ref, i, oth).start(priority=i % 2)
        h2 = _dot(x, wu_s[...])
        for i in range(bm):
            scatter(i, oth).start(priority=i % 2)
        ybuf[cur] = _dot((h1 * _sigmoid(h1) * h2).astype(BF16), wd_s[...])

    @pl.when(blk == n_used)
    def _():
        for i in range(bm):
            scatter(i, oth).start(priority=i % 2)
        for i in range(bm):
            scatter(i, oth).wait()


def moe(layer, hf, eidx, wts, w_gate, w_up, w_down, bm, precise):
    T, D = hf.shape
    A = T * TOP_K
    n_blocks = -(-A // bm) + N_EXPERTS
    rows = n_blocks * bm
    e_flat = eidx.reshape(A)
    onehot = (e_flat[:, None] == jnp.arange(N_EXPERTS, dtype=I32)[None, :]).astype(I32)
    csum = jnp.cumsum(onehot, axis=0)
    counts = csum[-1]
    rank = jnp.sum((csum - onehot) * onehot, axis=1)
    padded = (counts + bm - 1) // bm * bm
    pad_end = jnp.cumsum(padded)
    pad_start = pad_end - padded
    pos = jnp.sum(pad_start[None, :] * onehot, axis=1) + rank
    n_used = pad_end[-1] // bm
    blk_ids = jnp.arange(n_blocks, dtype=I32)
    blk_exp = jnp.minimum(jnp.sum((pad_end[None, :] <= (blk_ids * bm)[:, None]).astype(I32), axis=1), N_EXPERTS - 1)
    last_exp = blk_exp[jnp.maximum(n_used - 1, 0)]
    blk_exp = jnp.where(blk_ids < n_used, blk_exp, last_exp)
    meta = jnp.concatenate([layer.reshape(1), n_used.reshape(1).astype(I32), blk_exp])
    n_scr = 6 if precise else 3
    scr = []
    for shape in ((D, D_EXPERT), (D, D_EXPERT), (D_EXPERT, D)):
        scr += [pltpu.VMEM(shape, BF16)] * (n_scr // 3)
    a_ids = jnp.arange(A, dtype=I32)
    if not precise:
        y_rows = TOP_K * T + bm
        spare = TOP_K * T + jnp.arange(rows, dtype=I32) % bm
        base = jnp.stack([jnp.zeros((rows,), I32), spare], axis=1)
        vals = jnp.stack([a_ids // TOP_K, (a_ids % TOP_K) * T + a_ids // TOP_K], axis=1)
        tab = base.at[pos].set(vals)
        tok_tab = tab[:, 0].reshape(n_blocks, 1, bm)
        dst_tab = jnp.concatenate([spare[:bm], tab[:, 1]]).reshape(n_blocks + 1, 1, bm)
        smem = functools.partial(pl.BlockSpec, memory_space=pltpu.SMEM)
        return _pcall(
            functools.partial(_expert_gather_body, bm=bm), (n_blocks,),
            [smem((None, 1, bm), lambda b, m: (jnp.minimum(b + 1, n_blocks - 1), 0, 0)),
             smem((None, 1, bm), lambda b, m: (0, 0, 0)),
             smem((None, 1, bm), lambda b, m: (b, 0, 0)),
             pl.BlockSpec((None, None, D, D_EXPERT), lambda b, m: (m[0], m[2 + b], 0, 0)),
             pl.BlockSpec((None, None, D, D_EXPERT), lambda b, m: (m[0], m[2 + b], 0, 0)),
             pl.BlockSpec((None, None, D_EXPERT, D), lambda b, m: (m[0], m[2 + b], 0, 0)),
             pl.BlockSpec(memory_space=pl.ANY)],
            pl.BlockSpec(memory_space=pl.ANY),
            jax.ShapeDtypeStruct((y_rows, D), F32),
            scratch=scr + [pltpu.VMEM((2, bm, D), F32), pltpu.VMEM((2, bm, D), F32),
                           pltpu.SemaphoreType.DMA((2,)), pltpu.SemaphoreType.DMA((2,))],
            sem=("arbitrary",), name="experts_gather")(
                meta, tok_tab, tok_tab, dst_tab, w_gate, w_up, w_down, hf)
    row_tok = jnp.full((rows,), T, I32).at[pos].set(a_ids // TOP_K)
    row_w = jnp.zeros((rows,), F32).at[pos].set(wts.reshape(A))
    xs = jnp.concatenate([hf, jnp.zeros((1, D), hf.dtype)], axis=0)[row_tok]
    ys = _pcall(
        functools.partial(_expert_body, precise=precise), (n_blocks,),
        [pl.BlockSpec((bm, D), lambda b, m: (b, 0)),
         pl.BlockSpec((bm, 1), lambda b, m: (b, 0)),
         pl.BlockSpec((None, None, D, D_EXPERT), lambda b, m: (m[0], m[2 + b], 0, 0)),
         pl.BlockSpec((None, None, D, D_EXPERT), lambda b, m: (m[0], m[2 + b], 0, 0)),
         pl.BlockSpec((None, None, D_EXPERT, D), lambda b, m: (m[0], m[2 + b], 0, 0))],
        pl.BlockSpec((bm, D), lambda b, m: (b, 0)),
        jax.ShapeDtypeStruct((rows, D), F32),
        scratch=scr, sem=("arbitrary",), name="experts")(
            meta, xs, row_w.reshape(rows, 1), w_gate, w_up, w_down)
    return ys[pos.reshape(T, TOP_K).T.reshape(A)].reshape(TOP_K, T, D)


def _moe_residual_body(meta, x_ref, y0_ref, y1_ref, gt_ref, o_ref):
    o_ref[...] = x_ref[...] + gt_ref[...] * (y0_ref[...] + y1_ref[...])


def _moe_residual_w_body(meta, x_ref, y0_ref, y1_ref, gt_ref, w_ref, o_ref):
    w = w_ref[...]
    o_ref[...] = x_ref[...] + gt_ref[...] * (y0_ref[...] * w[:, 0:1] + y1_ref[...] * w[:, 1:2])


def moe_residual(x3, y2, gt, wts=None):
    B, S, D = x3.shape
    ts = _row_tile(S)
    meta = jnp.zeros((1,), I32)
    x_spec = pl.BlockSpec((None, ts, D), lambda b, i, m: (b, i, 0))
    gt_spec = pl.BlockSpec((None, 1, D), lambda b, i, m: (b, 0, 0))
    if y2.ndim == 2:
        per = S // ts
        body = _moe_residual_w_body
        in_specs = [x_spec,
                    pl.BlockSpec((ts, D), lambda b, i, m: (b * per + i, 0)),
                    pl.BlockSpec((ts, D), lambda b, i, m: (B * per + b * per + i, 0)),
                    gt_spec,
                    pl.BlockSpec((None, ts, TOP_K), lambda b, i, m: (b, i, 0))]
        args = (x3, y2, y2, gt, wts.reshape(B, S, TOP_K))
    else:
        y3 = y2.reshape(TOP_K, B, S, D)
        body = _moe_residual_body
        in_specs = [x_spec,
                    pl.BlockSpec((None, None, ts, D), lambda b, i, m: (0, b, i, 0)),
                    pl.BlockSpec((None, None, ts, D), lambda b, i, m: (1, b, i, 0)),
                    gt_spec]
        args = (x3, y3, y3, gt)
    return _pcall(
        body, (B, S // ts), in_specs, x_spec, jax.ShapeDtypeStruct((B, S, D), F32),
        sem=("parallel", "parallel"), name="moe_residual")(meta, *args)


def _rope_tables(pos):
    half = HEAD_DIM // 2
    inv = ROPE_THETA ** (-jnp.arange(half, dtype=F32) / half)
    ang = pos.astype(F32)[:, None] * inv[None, :]
    cos, sin = jnp.cos(ang), jnp.sin(ang)
    return jnp.concatenate([cos, cos], axis=1), jnp.concatenate([-sin, sin], axis=1)


def _layer(layer, x3, mod, params, rope, sample_state):
    (g_mix, g_ffn, w_in, w_gates, w_tail, b_igate, b_fgate, g_mlstm_out, w_att_out, w_mlstm_out,
     w_mix_out, wr_t, b_router, w_exp_gate, w_exp_up, w_exp_down) = params
    B, S, D = x3.shape
    T = B * S
    sample = sample_state is not None
    act = F32 if sample else BF16
    meta = layer.reshape(1)
    sh1, sc1, gt1, sh2, sc2, gt2 = [m.reshape(B, 1, D) for m in jnp.split(mod, 6, axis=-1)]

    wins = []
    if sample:
        h = norm_mod(meta, x3, g_mix, sc1, sh1, act).reshape(T, D)
        proj = linear(meta, h, w_in, 0, GATE_OFF, F32, True, rot=(rope[0], rope[1], 2 * ATT_WIDTH // 512))
        qkv, qkvm = proj[:, :3 * ATT_WIDTH], proj[:, 3 * ATT_WIDTH:]
        for g in range(len(ATT_GROUPS)):
            kg = qkv[:, ATT_WIDTH + g * ATT_OUT:ATT_WIDTH + (g + 1) * ATT_OUT]
            vg = qkv[:, 2 * ATT_WIDTH + g * ATT_OUT:2 * ATT_WIDTH + (g + 1) * ATT_OUT]
            wins.append(jnp.concatenate([kg, vg], axis=1).reshape(B, S, 2, ATT_HPG, HEAD_DIM))
    else:
        hs = norm_mod_streams(meta, x3, g_mix, sc1, sh1)
        h = hs[0]
        qkv_g = [linear(meta, hs[g], w_in, g * ATT_OUT, 3 * ATT_OUT, F32, False, rot=(rope[g][0], rope[g][1], 2),
                        col_step=len(ATT_GROUPS)) for g in range(len(ATT_GROUPS))]
        for g, (win, dil) in enumerate(ATT_GROUPS):
            keep = min(win, S)
            L = S // dil
            kv = qkv_g[g].reshape(B, dil, L, 3 * ATT_OUT)[:, :, L - keep // dil:, ATT_OUT:]
            wins.append(jnp.swapaxes(kv, 1, 2).reshape(B, keep, 2, ATT_HPG, HEAD_DIM))
    if not sample:
        qkvm = linear(meta, h, w_in, 3 * ATT_WIDTH, 3 * M_WIDTH, act, False)
    gates = linear(meta, h, w_gates, 0, LANES, F32, sample, tn=LANES)
    tail = linear(meta, h, w_tail, 0, M_WIDTH + 2 * D_MODEL, act, sample)

    if sample:
        caches, st_c, st_n, st_m = sample_state
        att = sample_attention(meta, qkv, caches)
        hm, c1, n1, m1 = mlstm_sample(meta, qkvm, gates, b_igate, b_fgate, tail, g_mlstm_out, st_c, st_n, st_m)
    else:
        outs, lses = zip(*[band_attention(qkv_g[g], B, S, g) for g in range(len(ATT_GROUPS))])
        att = attn_merge(outs, lses)
        hm, c1, n1, m1 = mlstm_prompt(meta, qkvm, gates, b_igate, b_fgate, tail, g_mlstm_out, B, S)

    merged = merge(meta, att, hm, w_att_out, w_mlstm_out, tail, act, sample)
    x2 = linear_residual(meta, merged, w_mix_out, x3.reshape(T, D), gt1, S, sample)
    x3 = x2.reshape(B, S, D)

    hf, eidx, wts = norm_route(meta, x3, g_ffn, sc2, sh2, wr_t, b_router, F32)
    bm = MOE_BLOCK_SAMPLE if sample else MOE_BLOCK_PROMPT
    y2 = moe(layer, hf, eidx, wts, w_exp_gate, w_exp_up, w_exp_down, bm, sample)
    x3 = moe_residual(x3, y2, gt2, None if sample else wts)
    return x3, (wins[0], wins[1], wins[2], c1, n1, m1)


def kernel(x_prompt, x_sample, cache_win_w128, cache_win_w512, cache_win_w2048, state_mlstm_c, state_mlstm_n,
           state_mlstm_m, c_prompt, c_sample, w_ada, b_ada, g_mix, g_ffn, w_in, b_igate, b_fgate, g_mlstm_out,
           w_att_out, w_mlstm_out, w_mix_out, w_router, b_router, w_exp_gate, w_exp_up, w_exp_down, g_final):
    B, S, D = x_prompt.shape
    Bd, Sd, _ = x_sample.shape
    assert Sd == 1 and D == D_MODEL and w_in.shape[2] == D_IN

    n_c = B + Bd
    c_rows = -(-n_c // SUBLANES) * SUBLANES
    c_all = jnp.pad(jnp.concatenate([c_prompt, c_sample], axis=0), ((0, c_rows - n_c), (0, 0)))
    mod = ada_modulation(c_all, w_ada, b_ada)

    w_gates = jnp.pad(w_in[:, :, GATE_OFF:TAIL_OFF], ((0, 0), (0, 0), (0, LANES - 2 * M_HEADS)))
    w_tail = w_in[:, :, TAIL_OFF:]
    params = (g_mix.reshape(DEPTH, 1, D), g_ffn.reshape(DEPTH, 1, D), w_in, w_gates, w_tail, b_igate, b_fgate,
              g_mlstm_out, w_att_out, w_mlstm_out, w_mix_out, w_router.T, b_router,
              w_exp_gate, w_exp_up, w_exp_down)
    rope_p = [_rope_tables((jnp.arange(S // dil, dtype=I32)[None, :] * dil
                            + jnp.arange(dil, dtype=I32)[:, None]).reshape(S)) for _, dil in ATT_GROUPS]
    rope_s = _rope_tables(jnp.full((Bd,), PAST_LEN, I32))
    caches = []
    for c, (win, dil) in zip((cache_win_w128, cache_win_w512, cache_win_w2048), ATT_GROUPS):
        assert c.shape[2] == win and win == ATT_BLOCK * dil
        caches.append(c[:, :, ::dil].reshape(DEPTH, Bd, ATT_BLOCK, 2 * ATT_OUT))
    sample_state = (caches, state_mlstm_c, state_mlstm_n, state_mlstm_m)

    def body(carry, layer):
        xp, xs = carry
        mod_l = lax.dynamic_index_in_dim(mod, layer, 0, keepdims=False)
        xp, out_p = _layer(layer, xp, mod_l[:B], params, rope_p, None)
        xs, out_s = _layer(layer, xs, mod_l[B:n_c], params, rope_s, sample_state)
        return (xp, xs), (out_p, out_s)

    (xp, xs), (out_p, out_s) = lax.scan(body, (x_prompt, x_sample), jnp.arange(DEPTH, dtype=I32))
    y_prompt = norm_final(xp, g_final)
    y_sample = norm_final(xs, g_final)
    return (y_prompt, y_sample) + tuple(out_p) + tuple(out_s)
```

```python
import functools

import numpy as np
import jax
import jax.numpy as jnp
from jax import lax
from jax.experimental import pallas as pl
from jax.experimental.pallas import tpu as pltpu

F32 = jnp.float32
BF16 = jnp.bfloat16
I32 = jnp.int32

D_MODEL = 2048
DEPTH = 4
PAST_LEN = 16384
HEAD_DIM = 128
ATT_GROUPS = ((128, 1), (512, 4), (2048, 16))
ATT_HPG = 4
ATT_WIDTH = len(ATT_GROUPS) * ATT_HPG * HEAD_DIM
ATT_OUT = ATT_HPG * HEAD_DIM
ATT_BLOCK = 128
ROPE_THETA = 10000.0
M_HEADS = 4
M_WIDTH = D_MODEL // 2
M_HD = M_WIDTH // M_HEADS
N_EXPERTS = 16
N_EGROUPS = 4
EPG = N_EXPERTS // N_EGROUPS
TOP_K = 2
D_EXPERT = D_MODEL // 4
EPS = 1e-6
NEG_INF = -1e30
D_IN = 3 * ATT_WIDTH + 3 * M_WIDTH + 2 * M_HEADS + M_WIDTH + 2 * D_MODEL
GATE_OFF = 3 * ATT_WIDTH + 3 * M_WIDTH
TAIL_OFF = GATE_OFF + 2 * M_HEADS

V7X_VMEM_BYTES = 64 * 1024 * 1024
LANES = 128
SUBLANES = 8
VMEM_LIMIT = 56 * 1024 * 1024

MLSTM_CHUNK = 256
MOE_BLOCK_PROMPT = 256
MOE_BLOCK_SAMPLE = 8


def _pcall(body, grid, in_specs, out_specs, out_shape, scratch=(), sem=None, name=None):
    return pl.pallas_call(
        body,
        grid_spec=pltpu.PrefetchScalarGridSpec(
            num_scalar_prefetch=1, grid=grid, in_specs=in_specs, out_specs=out_specs,
            scratch_shapes=list(scratch)),
        out_shape=out_shape,
        compiler_params=pltpu.CompilerParams(dimension_semantics=sem, vmem_limit_bytes=VMEM_LIMIT),
        name=name)


def _split(x):
    hi = x.astype(BF16)
    lo = (x - hi.astype(F32)).astype(BF16)
    return hi, lo


def _dot(a, b):
    return jnp.dot(a, b, preferred_element_type=F32)


def _dot_nt(a, b):
    return lax.dot_general(a, b, (((1,), (1,)), ((), ())), preferred_element_type=F32)


def _dot_tn(a, b):
    return lax.dot_general(a, b, (((0,), (0,)), ((), ())), preferred_element_type=F32)


def _dot3(a_hi, a_lo, w_hi, w_lo):
    return _dot(a_hi, w_hi) + (_dot(a_hi, w_lo) + _dot(a_lo, w_hi))


def _sigmoid(x):
    return 1.0 / (1.0 + jnp.exp(-x))


def _log_sigmoid(x):
    return jnp.minimum(x, 0.0) - jnp.log1p(jnp.exp(-jnp.abs(x)))


def _ada_body(meta, c_ref, w_ref, b_ref, o_ref):
    c = c_ref[...]
    a_hi, a_lo = _split(c * _sigmoid(c))
    w_hi, w_lo = _split(w_ref[...])
    o_ref[...] = _dot3(a_hi, a_lo, w_hi, w_lo) + b_ref[...]


def ada_modulation(c_all, w_ada, b_ada):
    rows = c_all.shape[0]
    n_out = w_ada.shape[2]
    tn = 1024
    meta = jnp.zeros((1,), I32)
    return _pcall(
        _ada_body, (DEPTH, n_out // tn),
        [pl.BlockSpec((rows, D_MODEL), lambda l, j, m: (0, 0)),
         pl.BlockSpec((None, D_MODEL, tn), lambda l, j, m: (l, 0, j)),
         pl.BlockSpec((None, 1, tn), lambda l, j, m: (l, 0, j))],
        pl.BlockSpec((None, rows, tn), lambda l, j, m: (l, 0, j)),
        jax.ShapeDtypeStruct((DEPTH, rows, n_out), F32),
        sem=("parallel", "parallel"), name="ada_mod")(meta, c_all, w_ada, b_ada.reshape(DEPTH, 1, n_out))


def _rms(x, g):
    return x * lax.rsqrt(jnp.mean(x * x, axis=-1, keepdims=True) + EPS) * g


def _norm_mod_body(meta, x_ref, g_ref, sc_ref, sh_ref, o_ref):
    y = _rms(x_ref[...], g_ref[...])
    o_ref[...] = (y * (1.0 + sc_ref[...]) + sh_ref[...]).astype(o_ref.dtype)


def _norm_plain_body(meta, x_ref, g_ref, o_ref):
    o_ref[...] = _rms(x_ref[...], g_ref[...]).astype(o_ref.dtype)


def _row_tile(S):
    return min(S, 512)


def norm_mod(meta, x3, g_all, sc, sh, out_dtype):
    B, S, D = x3.shape
    ts = _row_tile(S)
    return _pcall(
        _norm_mod_body, (B, S // ts),
        [pl.BlockSpec((None, ts, D), lambda b, i, m: (b, i, 0)),
         pl.BlockSpec((None, 1, D), lambda b, i, m: (m[0], 0, 0)),
         pl.BlockSpec((None, 1, D), lambda b, i, m: (b, 0, 0)),
         pl.BlockSpec((None, 1, D), lambda b, i, m: (b, 0, 0))],
        pl.BlockSpec((None, ts, D), lambda b, i, m: (b, i, 0)),
        jax.ShapeDtypeStruct((B, S, D), out_dtype),
        sem=("parallel", "parallel"), name="norm_mod")(meta, x3, g_all, sc, sh)


def _norm_streams_body(meta, x_ref, g_ref, sc_ref, sh_ref, o0_ref, o1_ref, o2_ref, y_s, *, ts):
    y = _rms(x_ref[...], g_ref[...]) * (1.0 + sc_ref[...]) + sh_ref[...]
    o0_ref[...] = y.astype(o0_ref.dtype)
    n_slabs = y.shape[1] // LANES
    for c in range(n_slabs):
        y_s[c] = y[:, c * LANES:(c + 1) * LANES]
    for o_ref, (_, dil) in ((o1_ref, ATT_GROUPS[1]), (o2_ref, ATT_GROUPS[2])):
        for r in range(dil):
            for c in range(n_slabs):
                o_ref[r, :, c * LANES:(c + 1) * LANES] = (
                    y_s[c, pl.ds(r, ts // dil, stride=dil), :].astype(o_ref.dtype))


def norm_mod_streams(meta, x3, g_all, sc, sh):
    B, S, D = x3.shape
    ts = _row_tile(S)
    d1, d2 = ATT_GROUPS[1][1], ATT_GROUPS[2][1]
    outs = _pcall(
        functools.partial(_norm_streams_body, ts=ts), (B, S // ts),
        [pl.BlockSpec((None, ts, D), lambda b, i, m: (b, i, 0)),
         pl.BlockSpec((None, 1, D), lambda b, i, m: (m[0], 0, 0)),
         pl.BlockSpec((None, 1, D), lambda b, i, m: (b, 0, 0)),
         pl.BlockSpec((None, 1, D), lambda b, i, m: (b, 0, 0))],
        [pl.BlockSpec((None, ts, D), lambda b, i, m: (b, i, 0)),
         pl.BlockSpec((None, d1, ts // d1, D), lambda b, i, m: (b, 0, i, 0)),
         pl.BlockSpec((None, d2, ts // d2, D), lambda b, i, m: (b, 0, i, 0))],
        [jax.ShapeDtypeStruct((B, S, D), BF16),
         jax.ShapeDtypeStruct((B, d1, S // d1, D), BF16),
         jax.ShapeDtypeStruct((B, d2, S // d2, D), BF16)],
        scratch=[pltpu.VMEM((D // LANES, ts, LANES), F32)],
        sem=("parallel", "parallel"), name="norm_mod_streams")(meta, x3, g_all, sc, sh)
    return [o.reshape(B * S, D) for o in outs]


def norm_final(x3, g):
    B, S, D = x3.shape
    ts = _row_tile(S)
    meta = jnp.zeros((1,), I32)
    return _pcall(
        _norm_plain_body, (B, S // ts),
        [pl.BlockSpec((None, ts, D), lambda b, i, m: (b, i, 0)),
         pl.BlockSpec((1, D), lambda b, i, m: (0, 0))],
        pl.BlockSpec((None, ts, D), lambda b, i, m: (b, i, 0)),
        jax.ShapeDtypeStruct((B, S, D), F32),
        sem=("parallel", "parallel"), name="norm_final")(meta, x3, g.reshape(1, D))


def _matmul_tile(a_ref, w_ref, precise):
    w = w_ref[...]
    if precise:
        a_hi, a_lo = _split(a_ref[...])
        w_hi, w_lo = _split(w)
        return _dot3(a_hi, a_lo, w_hi, w_lo)
    return _dot(a_ref[...], w.astype(BF16))


def _linear_body(meta, a_ref, w_ref, o_ref, *, precise):
    o_ref[...] = _matmul_tile(a_ref, w_ref, precise).astype(o_ref.dtype)


def _linear_rot_body(meta, a_ref, w_ref, cos_ref, sin_ref, o_ref, *, precise, n_rot, tn):
    acc = _matmul_tile(a_ref, w_ref, precise)
    j = pl.program_id(1)

    @pl.when(j < n_rot)
    def _():
        c = cos_ref[...]
        s = sin_ref[...]
        for h in range(tn // HEAD_DIM):
            hs = slice(h * HEAD_DIM, (h + 1) * HEAD_DIM)
            x = acc[:, hs]
            o_ref[:, hs] = (x * c + pltpu.roll(x, HEAD_DIM // 2, 1) * s).astype(o_ref.dtype)

    @pl.when(j >= n_rot)
    def _():
        o_ref[...] = acc.astype(o_ref.dtype)


def _linear_res_body(meta, a_ref, w_ref, x_ref, gt_ref, o_ref, *, precise):
    o_ref[...] = x_ref[...] + gt_ref[...] * _matmul_tile(a_ref, w_ref, precise)


def _m_tile(T):
    return min(T, 2048)


def linear(meta, a, w, col_off, n_out, out_dtype, precise, tn=512, rot=None, col_step=1):
    T, K = a.shape
    tm = _m_tile(T)
    off = col_off // tn
    in_specs = [pl.BlockSpec((tm, K), lambda i, j, m: (i, 0)),
                pl.BlockSpec((None, K, tn), lambda i, j, m: (m[0], 0, off + j * col_step))]
    args = [a, w]
    if rot is None:
        body = functools.partial(_linear_body, precise=precise)
    else:
        cos_t, sin_t, n_rot = rot
        nper = cos_t.shape[0] // tm
        in_specs += [pl.BlockSpec((tm, HEAD_DIM), lambda i, j, m: (i % nper, 0)),
                     pl.BlockSpec((tm, HEAD_DIM), lambda i, j, m: (i % nper, 0))]
        args += [cos_t, sin_t]
        body = functools.partial(_linear_rot_body, precise=precise, n_rot=n_rot, tn=tn)
    return _pcall(
        body, (T // tm, n_out // tn), in_specs,
        pl.BlockSpec((tm, tn), lambda i, j, m: (i, j)),
        jax.ShapeDtypeStruct((T, n_out), out_dtype),
        sem=("parallel", "arbitrary"), name="linear")(meta, *args)


def linear_residual(meta, a, w, x2, gt, rows_per_batch, precise, tn=512):
    T, K = a.shape
    D = x2.shape[1]
    tm = _m_tile(T)
    if rows_per_batch == 1:
        gt_arr = gt.reshape(T, D)
        gt_spec = pl.BlockSpec((tm, tn), lambda i, j, m: (i, j))
    else:
        per = rows_per_batch // tm
        gt_arr = gt
        gt_spec = pl.BlockSpec((None, 1, tn), lambda i, j, m: (i // per, 0, j))
    return _pcall(
        functools.partial(_linear_res_body, precise=precise), (T // tm, D // tn),
        [pl.BlockSpec((tm, K), lambda i, j, m: (i, 0)),
         pl.BlockSpec((None, K, tn), lambda i, j, m: (m[0], 0, j)),
         pl.BlockSpec((tm, tn), lambda i, j, m: (i, j)),
         gt_spec],
        pl.BlockSpec((tm, tn), lambda i, j, m: (i, j)),
        jax.ShapeDtypeStruct((T, D), F32),
        sem=("parallel", "arbitrary"), name="linear_residual")(meta, a, w, x2, gt_arr)


def _attn_body(meta, q_ref, k_ref, v_ref, o_ref, lse_ref, *, nb):
    scale = HEAD_DIM ** -0.5
    qi = lax.broadcasted_iota(I32, (ATT_BLOCK, ATT_BLOCK), 0)
    kj = lax.broadcasted_iota(I32, (ATT_BLOCK, ATT_BLOCK), 1)
    m_cur = kj <= qi
    m_prev = qi <= kj

    def blk(n, carry):
        qs = pl.multiple_of(n * ATT_BLOCK, ATT_BLOCK)
        ps = pl.multiple_of(jnp.maximum(n - 1, 0) * ATT_BLOCK, ATT_BLOCK)
        prev_cap = jnp.where(n > 0, -NEG_INF, NEG_INF)
        for h in range(ATT_HPG):
            hs = slice(h * HEAD_DIM, (h + 1) * HEAD_DIM)
            q = q_ref[pl.ds(qs, ATT_BLOCK), hs].astype(BF16)
            kc = k_ref[pl.ds(qs, ATT_BLOCK), hs].astype(BF16)
            kp = k_ref[pl.ds(ps, ATT_BLOCK), hs].astype(BF16)
            vc = v_ref[pl.ds(qs, ATT_BLOCK), hs].astype(BF16)
            vp = v_ref[pl.ds(ps, ATT_BLOCK), hs].astype(BF16)
            sc = jnp.where(m_cur, _dot_nt(q, kc) * scale, NEG_INF)
            sp = jnp.minimum(jnp.where(m_prev, _dot_nt(q, kp) * scale, NEG_INF), prev_cap)
            mx = jnp.maximum(jnp.max(sc, axis=1, keepdims=True), jnp.max(sp, axis=1, keepdims=True))
            pc = jnp.exp(sc - mx)
            pp = jnp.exp(sp - mx)
            den = jnp.sum(pc, axis=1, keepdims=True) + jnp.sum(pp, axis=1, keepdims=True)
            o = (_dot(pc.astype(BF16), vc) + _dot(pp.astype(BF16), vp)) / den
            o_ref[pl.ds(qs, ATT_BLOCK), hs] = o.astype(o_ref.dtype)
            lse_ref[pl.ds(qs, ATT_BLOCK), hs] = jnp.broadcast_to(mx + jnp.log(den), (ATT_BLOCK, HEAD_DIM))
        return carry

    lax.fori_loop(0, nb, blk, 0)


def band_attention(qkv_g, B, S, g):
    _, dil = ATT_GROUPS[g]
    L = S // dil
    assert L % ATT_BLOCK == 0
    qv = qkv_g.reshape(B, dil, L, 3 * ATT_OUT)
    meta = jnp.zeros((1,), I32)
    o, lse = _pcall(
        functools.partial(_attn_body, nb=L // ATT_BLOCK), (B, dil),
        [pl.BlockSpec((None, None, L, ATT_OUT), lambda b, r, m: (b, r, 0, 0)),
         pl.BlockSpec((None, None, L, ATT_OUT), lambda b, r, m: (b, r, 0, 1)),
         pl.BlockSpec((None, None, L, ATT_OUT), lambda b, r, m: (b, r, 0, 2))],
        [pl.BlockSpec((None, L, ATT_OUT), lambda b, r, m: (b, 0, r)),
         pl.BlockSpec((None, L, ATT_OUT), lambda b, r, m: (b, 0, r))],
        [jax.ShapeDtypeStruct((B, L, dil * ATT_OUT), BF16),
         jax.ShapeDtypeStruct((B, L, dil * ATT_OUT), F32)],
        sem=("parallel", "parallel"), name="band_attention")(meta, qv, qv, qv)
    return o.reshape(B * S, ATT_OUT), lse.reshape(B * S, ATT_OUT)


def _attn_merge_body(meta, o0, o1, o2, l0, l1, l2, att_ref):
    a0, a1, a2 = l0[...], l1[...], l2[...]
    mx = jnp.maximum(jnp.maximum(a0, a1), a2)
    e0, e1, e2 = jnp.exp(a0 - mx), jnp.exp(a1 - mx), jnp.exp(a2 - mx)
    num = e0 * o0[...].astype(F32) + e1 * o1[...].astype(F32) + e2 * o2[...].astype(F32)
    att_ref[...] = (num / (e0 + e1 + e2)).astype(att_ref.dtype)


def attn_merge(outs, lses):
    T = outs[0].shape[0]
    tm = _m_tile(T)
    spec = pl.BlockSpec((tm, ATT_OUT), lambda i, m: (i, 0))
    meta = jnp.zeros((1,), I32)
    return _pcall(_attn_merge_body, (T // tm,), [spec] * 6, spec,
                  jax.ShapeDtypeStruct((T, ATT_OUT), BF16),
                  sem=("parallel",), name="attn_merge")(meta, *outs, *lses)


def _attn_sample_body(meta, q_ref, c0_ref, c1_ref, c2_ref, att_ref):
    scale = HEAD_DIM ** -0.5
    row = q_ref[...]
    caches = (c0_ref, c1_ref, c2_ref)
    for h in range(ATT_HPG):
        outs, lses = [], []
        for g in range(len(ATT_GROUPS)):
            col = (g * ATT_HPG + h) * HEAD_DIM
            q = row[:, col:col + HEAD_DIM]
            k_new = row[:, ATT_WIDTH + col:ATT_WIDTH + col + HEAD_DIM]
            v_new = row[:, 2 * ATT_WIDTH + col:2 * ATT_WIDTH + col + HEAD_DIM]
            kc = caches[g][:, h * HEAD_DIM:(h + 1) * HEAD_DIM]
            vc = caches[g][:, ATT_OUT + h * HEAD_DIM:ATT_OUT + (h + 1) * HEAD_DIM]
            s = jnp.sum(kc * q, axis=1, keepdims=True) * scale
            s0 = jnp.sum(k_new * q, axis=1, keepdims=True) * scale
            mx = jnp.maximum(jnp.max(s, axis=0, keepdims=True), s0)
            p = jnp.exp(s - mx)
            p0 = jnp.exp(s0 - mx)
            den = jnp.sum(p, axis=0, keepdims=True) + p0
            outs.append((jnp.sum(vc * p, axis=0, keepdims=True) + p0 * v_new) / den)
            lses.append(mx + jnp.log(den))
        mxl = jnp.maximum(jnp.maximum(lses[0], lses[1]), lses[2])
        es = [jnp.exp(l - mxl) for l in lses]
        num = es[0] * outs[0] + es[1] * outs[1] + es[2] * outs[2]
        att_ref[:, h * HEAD_DIM:(h + 1) * HEAD_DIM] = num / (es[0] + es[1] + es[2])


def sample_attention(meta, qkv, caches):
    Bd = qkv.shape[0]
    wcols = 3 * ATT_WIDTH
    in_specs = [pl.BlockSpec((None, 1, wcols), lambda b, m: (b, 0, 0))]
    args = [qkv.reshape(Bd, 1, wcols)]
    for g in range(len(ATT_GROUPS)):
        args.append(caches[g])
        in_specs.append(pl.BlockSpec((None, None, ATT_BLOCK, 2 * ATT_OUT), lambda b, m: (m[0], b, 0, 0)))
    att = _pcall(
        _attn_sample_body, (Bd,), in_specs,
        pl.BlockSpec((None, 1, ATT_OUT), lambda b, m: (b, 0, 0)),
        jax.ShapeDtypeStruct((Bd, 1, ATT_OUT), F32),
        sem=("parallel",), name="sample_attention")(meta, *args)
    return att.reshape(Bd, ATT_OUT)


def _mlstm_body(meta, q_ref, k_ref, v_ref, gc_ref, gr_ref, bc_ref, br_ref, og_ref, g_ref,
                hm_ref, caug_ref, m_ref, c_s, m_s, *, Lc, nc):
    h = pl.program_id(1)
    c = pl.program_id(2)

    @pl.when(c == 0)
    def _():
        c_s[...] = jnp.zeros_like(c_s)
        m_s[...] = jnp.zeros_like(m_s)

    gcol = gc_ref[...] + bc_ref[...]
    lane = lax.broadcasted_iota(I32, gcol.shape, 1)
    i_col = jnp.sum(jnp.where(lane == h, gcol, 0.0), axis=1, keepdims=True)
    f_col = jnp.sum(jnp.where(lane == h + M_HEADS, gcol, 0.0), axis=1, keepdims=True)
    grow = gr_ref[...] + br_ref[...]
    sub = lax.broadcasted_iota(I32, grow.shape, 0)
    i_row = jnp.sum(jnp.where(sub == h, grow, 0.0), axis=0, keepdims=True)
    f_row = jnp.sum(jnp.where(sub == h + M_HEADS, grow, 0.0), axis=0, keepdims=True)
    lf_col = _log_sigmoid(f_col)
    lf_row = _log_sigmoid(f_row)

    t_i = lax.broadcasted_iota(I32, (Lc, Lc), 0)
    s_i = lax.broadcasted_iota(I32, (Lc, Lc), 1)
    causal = s_i <= t_i
    b_col = jnp.sum(jnp.where(causal, lf_row, 0.0), axis=1, keepdims=True)
    b_row = jnp.sum(jnp.where(t_i <= s_i, lf_col, 0.0), axis=0, keepdims=True)
    a_row = i_row - b_row
    cmax_col = jnp.max(jnp.where(causal, a_row, -jnp.inf), axis=1, keepdims=True)
    m_prev = m_s[0:1, 0:1]
    m_t = b_col + jnp.maximum(m_prev, cmax_col)
    inter = jnp.exp(m_prev + b_col - m_t)
    dmat = jnp.exp(jnp.where(causal, a_row + (b_col - m_t), NEG_INF))

    q = q_ref[...]
    k = k_ref[...]
    v = v_ref[...]
    kscale = M_HD ** -0.5
    one_col = jnp.where(lax.broadcasted_iota(I32, (Lc, LANES), 1) == 0, 1.0, 0.0).astype(BF16)
    v_aug = jnp.concatenate([v, one_col], axis=1)
    w = _dot_nt(q, k) * kscale * dmat
    caug = c_s[...]
    num = inter * _dot(q, caug.astype(BF16)) + _dot(w.astype(BF16), v_aug)
    den = num[:, M_HD:M_HD + 1]
    hh = num[:, :M_HD] / jnp.maximum(jnp.abs(den), jnp.exp(-m_t))

    b_last = jnp.sum(lf_row, axis=1, keepdims=True)
    m_last = b_last + jnp.maximum(m_prev, jnp.max(a_row, axis=1, keepdims=True))
    decay = jnp.exp(m_prev + b_last - m_last)
    wk_col = jnp.exp(i_col + b_last - b_col - m_last)
    kw = (k.astype(F32) * (wk_col * kscale)).astype(BF16)
    c_new = decay * caug + _dot_tn(kw, v_aug)
    c_s[...] = c_new
    m_s[...] = jnp.broadcast_to(m_last, m_s.shape)

    y = _rms(hh, g_ref[...]) * _sigmoid(og_ref[...].astype(F32))
    hm_ref[...] = y.astype(hm_ref.dtype)

    @pl.when(c == nc - 1)
    def _():
        caug_ref[...] = c_new
        m_ref[...] = jnp.broadcast_to(m_last, m_ref.shape)


def mlstm_prompt(meta, qkvm, gates, b_i, b_f, tail, g_out, B, S):
    Lc = min(MLSTM_CHUNK, S)
    nc = S // Lc
    caw = M_HD + LANES
    q3 = qkvm.reshape(B, S, 3 * M_WIDTH)
    g3 = gates.reshape(B, S, LANES)
    gr = jnp.swapaxes(g3[:, :, :SUBLANES], 1, 2)
    bias = jnp.concatenate([b_i, b_f], axis=1)
    bc = jnp.pad(bias, ((0, 0), (0, LANES - 2 * M_HEADS))).reshape(DEPTH, 1, LANES)
    br = bias.reshape(DEPTH, 2 * M_HEADS, 1)
    t3 = tail.reshape(B, S, tail.shape[1])
    hm, caug, mfin = _pcall(
        functools.partial(_mlstm_body, Lc=Lc, nc=nc), (B, M_HEADS, nc),
        [pl.BlockSpec((None, Lc, M_HD), lambda b, h, c, m: (b, c, h)),
         pl.BlockSpec((None, Lc, M_HD), lambda b, h, c, m: (b, c, M_HEADS + h)),
         pl.BlockSpec((None, Lc, M_HD), lambda b, h, c, m: (b, c, 2 * M_HEADS + h)),
         pl.BlockSpec((None, Lc, LANES), lambda b, h, c, m: (b, c, 0)),
         pl.BlockSpec((None, SUBLANES, Lc), lambda b, h, c, m: (b, 0, c)),
         pl.BlockSpec((None, 1, LANES), lambda b, h, c, m: (m[0], 0, 0)),
         pl.BlockSpec((None, 2 * M_HEADS, 1), lambda b, h, c, m: (m[0], 0, 0)),
         pl.BlockSpec((None, Lc, M_HD), lambda b, h, c, m: (b, c, h)),
         pl.BlockSpec((None, 1, M_HD), lambda b, h, c, m: (m[0], 0, h))],
        [pl.BlockSpec((None, Lc, M_HD), lambda b, h, c, m: (b, c, h)),
         pl.BlockSpec((None, None, M_HD, caw), lambda b, h, c, m: (b, h, 0, 0)),
         pl.BlockSpec((None, None, SUBLANES, LANES), lambda b, h, c, m: (b, h, 0, 0))],
        [jax.ShapeDtypeStruct((B, S, M_WIDTH), BF16),
         jax.ShapeDtypeStruct((B, M_HEADS, M_HD, caw), F32),
         jax.ShapeDtypeStruct((B, M_HEADS, SUBLANES, LANES), F32)],
        scratch=[pltpu.VMEM((M_HD, caw), F32), pltpu.VMEM((SUBLANES, LANES), F32)],
        sem=("parallel", "parallel", "arbitrary"), name="mlstm_prompt")(
            meta, q3, q3, q3, g3, gr, bc, br, t3, g_out.reshape(DEPTH, 1, M_WIDTH))
    return (hm.reshape(B * S, M_WIDTH), caug[..., :M_HD], caug[..., M_HD], mfin[:, :, 0, 0])


def _mlstm_sample_body(meta, x_ref, g_ref, bc_ref, og_ref, gout_ref, c0_ref, n0_ref, m0_ref,
                       hm_ref, c1_ref, n1_ref, m1_ref):
    row = x_ref[...]
    gates = g_ref[...] + bc_ref[...]
    eye = lax.broadcasted_iota(I32, (M_HD, M_HD), 0) == lax.broadcasted_iota(I32, (M_HD, M_HD), 1)
    kscale = M_HD ** -0.5
    for h in range(M_HEADS):
        hs = slice(h * M_HD, (h + 1) * M_HD)
        q = row[:, h * M_HD:(h + 1) * M_HD]
        k = row[:, M_WIDTH + h * M_HD:M_WIDTH + (h + 1) * M_HD] * kscale
        v = row[:, 2 * M_WIDTH + h * M_HD:2 * M_WIDTH + (h + 1) * M_HD]
        q_col = jnp.sum(jnp.where(eye, q, 0.0), axis=1, keepdims=True)
        k_col = jnp.sum(jnp.where(eye, k, 0.0), axis=1, keepdims=True)
        ii = gates[:, h:h + 1]
        lf = _log_sigmoid(gates[:, M_HEADS + h:M_HEADS + h + 1])
        C = c0_ref[h]
        n = n0_ref[h:h + 1, :]
        m = m0_ref[:, h:h + 1]
        a = ii - lf
        m_t = lf + jnp.maximum(m, a)
        inter = jnp.exp(m + lf - m_t)
        dm = jnp.exp(a + (lf - m_t))
        w = jnp.sum(q * k, axis=1, keepdims=True) * dm
        num = inter * jnp.sum(C * q_col, axis=0, keepdims=True) + w * v
        den = inter * jnp.sum(q * n, axis=1, keepdims=True) + w
        hh = num / jnp.maximum(jnp.abs(den), jnp.exp(-m_t))
        wk = jnp.exp(ii + lf - lf - m_t)
        c1_ref[h] = inter * C + (wk * k_col) * v
        n1_ref[h:h + 1, :] = inter * n + wk * k
        m1_ref[:, h:h + 1] = m_t
        y = _rms(hh, gout_ref[:, hs]) * _sigmoid(og_ref[:, hs])
        hm_ref[:, hs] = y


def mlstm_sample(meta, qkvm, gates, b_i, b_f, tail, g_out, st_c, st_n, st_m):
    Bd = qkvm.shape[0]
    bias = jnp.concatenate([b_i, b_f], axis=1)
    bc = jnp.pad(bias, ((0, 0), (0, LANES - 2 * M_HEADS))).reshape(DEPTH, 1, LANES)
    tw = tail.shape[1]
    hm, c1, n1, m1 = _pcall(
        _mlstm_sample_body, (Bd,),
        [pl.BlockSpec((None, 1, 3 * M_WIDTH), lambda b, m: (b, 0, 0)),
         pl.BlockSpec((None, 1, LANES), lambda b, m: (b, 0, 0)),
         pl.BlockSpec((None, 1, LANES), lambda b, m: (m[0], 0, 0)),
         pl.BlockSpec((None, 1, M_WIDTH), lambda b, m: (b, 0, 0)),
         pl.BlockSpec((None, 1, M_WIDTH), lambda b, m: (m[0], 0, 0)),
         pl.BlockSpec((None, None, M_HEADS, M_HD, M_HD), lambda b, m: (m[0], b, 0, 0, 0)),
         pl.BlockSpec((None, None, M_HEADS, M_HD), lambda b, m: (m[0], b, 0, 0)),
         pl.BlockSpec((None, None, 1, M_HEADS), lambda b, m: (m[0], b, 0, 0))],
        [pl.BlockSpec((None, 1, M_WIDTH), lambda b, m: (b, 0, 0)),
         pl.BlockSpec((None, M_HEADS, M_HD, M_HD), lambda b, m: (b, 0, 0, 0)),
         pl.BlockSpec((None, M_HEADS, M_HD), lambda b, m: (b, 0, 0)),
         pl.BlockSpec((None, 1, M_HEADS), lambda b, m: (b, 0, 0))],
        [jax.ShapeDtypeStruct((Bd, 1, M_WIDTH), F32),
         jax.ShapeDtypeStruct((Bd, M_HEADS, M_HD, M_HD), F32),
         jax.ShapeDtypeStruct((Bd, M_HEADS, M_HD), F32),
         jax.ShapeDtypeStruct((Bd, 1, M_HEADS), F32)],
        sem=("parallel",), name="mlstm_sample")(
            meta, qkvm.reshape(Bd, 1, 3 * M_WIDTH), gates.reshape(Bd, 1, LANES), bc,
            tail.reshape(Bd, 1, tw), g_out.reshape(DEPTH, 1, M_WIDTH),
            st_c, st_n, st_m.reshape(DEPTH, Bd, 1, M_HEADS))
    return hm.reshape(Bd, M_WIDTH), c1, n1, m1.reshape(Bd, M_HEADS)


def _merge_body(meta, att_ref, hm_ref, wa_ref, wm_ref, ga_ref, gb_ref, o_ref, *, precise):
    ya = _matmul_tile(att_ref, wa_ref, precise)
    ym = _matmul_tile(hm_ref, wm_ref, precise)
    out = _sigmoid(ga_ref[...].astype(F32)) * ya + _sigmoid(gb_ref[...].astype(F32)) * ym
    o_ref[...] = out.astype(o_ref.dtype)


def merge(meta, att, hm, w_att_out, w_mlstm_out, tail, out_dtype, precise, tn=512):
    T = att.shape[0]
    tm = _m_tile(T)
    ga_off = M_WIDTH // tn
    gb_off = (M_WIDTH + D_MODEL) // tn
    return _pcall(
        functools.partial(_merge_body, precise=precise), (T // tm, D_MODEL // tn),
        [pl.BlockSpec((tm, ATT_OUT), lambda i, j, m: (i, 0)),
         pl.BlockSpec((tm, M_WIDTH), lambda i, j, m: (i, 0)),
         pl.BlockSpec((None, ATT_OUT, tn), lambda i, j, m: (m[0], 0, j)),
         pl.BlockSpec((None, M_WIDTH, tn), lambda i, j, m: (m[0], 0, j)),
         pl.BlockSpec((tm, tn), lambda i, j, m: (i, ga_off + j)),
         pl.BlockSpec((tm, tn), lambda i, j, m: (i, gb_off + j))],
        pl.BlockSpec((tm, tn), lambda i, j, m: (i, j)),
        jax.ShapeDtypeStruct((T, D_MODEL), out_dtype),
        sem=("parallel", "arbitrary"), name="merge")(meta, att, hm, w_att_out, w_mlstm_out, tail, tail)


def _top2_sum(a, b, c, d):
    hi1, lo1 = jnp.maximum(a, b), jnp.minimum(a, b)
    hi2, lo2 = jnp.maximum(c, d), jnp.minimum(c, d)
    return jnp.maximum(hi1, hi2) + jnp.maximum(jnp.minimum(hi1, hi2), jnp.maximum(lo1, lo2))


def _route_rows(s, sb):
    rows = [sb[e:e + 1, :] for e in range(N_EXPERTS)]
    urows = [s[e:e + 1, :] for e in range(N_EXPERTS)]
    gs = [_top2_sum(*rows[g * EPG:(g + 1) * EPG]) for g in range(N_EGROUPS)]
    best = gs[0]
    gsel = jnp.zeros(best.shape, I32)
    for g in range(1, N_EGROUPS):
        upd = gs[g] > best
        gsel = jnp.where(upd, g, gsel)
        best = jnp.where(upd, gs[g], best)
    vals, uvals = [], []
    for i in range(EPG):
        v, u = rows[i], urows[i]
        for g in range(1, N_EGROUPS):
            v = jnp.where(gsel == g, rows[g * EPG + i], v)
            u = jnp.where(gsel == g, urows[g * EPG + i], u)
        vals.append(v)
        uvals.append(u)
    b1, i1, u1 = vals[0], jnp.zeros(best.shape, I32), uvals[0]
    for i in range(1, EPG):
        upd = vals[i] > b1
        i1 = jnp.where(upd, i, i1)
        u1 = jnp.where(upd, uvals[i], u1)
        b1 = jnp.where(upd, vals[i], b1)
    b2 = jnp.full(best.shape, -jnp.inf, F32)
    i2 = jnp.zeros(best.shape, I32)
    u2 = jnp.zeros(best.shape, F32)
    for i in range(EPG):
        upd = (i1 != i) & (vals[i] > b2)
        i2 = jnp.where(upd, i, i2)
        u2 = jnp.where(upd, uvals[i], u2)
        b2 = jnp.where(upd, vals[i], b2)
    tot = u1 + u2
    eidx = jnp.concatenate([gsel * EPG + i1, gsel * EPG + i2], axis=0)
    wts = jnp.concatenate([u1 / tot, u2 / tot], axis=0)
    return eidx, wts


def _norm_route_body(meta, x_ref, g_ref, sc_ref, sh_ref, wr_ref, br_ref, hf_ref, e_ref, w_ref, *, on_mxu):
    hf = _rms(x_ref[...], g_ref[...]) * (1.0 + sc_ref[...]) + sh_ref[...]
    if len(hf_ref.shape) == 3:
        for c in range(hf_ref.shape[1]):
            hf_ref[:, c, :] = hf[:, c * LANES:(c + 1) * LANES].astype(hf_ref.dtype)
    else:
        hf_ref[...] = hf.astype(hf_ref.dtype)
    wr = wr_ref[...]
    if on_mxu:
        h_hi, h_lo = _split(hf)
        w_hi, w_lo = _split(wr)
        logits = _dot_nt(w_hi, h_hi) + (_dot_nt(w_hi, h_lo) + _dot_nt(w_lo, h_hi))
    else:
        logits = jnp.sum(wr * hf, axis=1, keepdims=True)
    s = _sigmoid(logits)
    eidx, wts = _route_rows(s, s + br_ref[...])
    e_ref[...] = eidx
    w_ref[...] = wts


def norm_route(meta, x3, g_all, sc, sh, wr_t, b_router, hf_dtype, slabs=False):
    B, S, D = x3.shape
    ts = _row_tile(S)
    if slabs:
        hf_spec = pl.BlockSpec((None, ts, D // LANES, LANES), lambda b, i, m: (b, i, 0, 0))
        hf_shape = (B, S, D // LANES, LANES)
    else:
        hf_spec = pl.BlockSpec((None, ts, D), lambda b, i, m: (b, i, 0))
        hf_shape = (B, S, D)
    hf, eidx, wts = _pcall(
        functools.partial(_norm_route_body, on_mxu=ts >= LANES), (B, S // ts),
        [pl.BlockSpec((None, ts, D), lambda b, i, m: (b, i, 0)),
         pl.BlockSpec((None, 1, D), lambda b, i, m: (m[0], 0, 0)),
         pl.BlockSpec((None, 1, D), lambda b, i, m: (b, 0, 0)),
         pl.BlockSpec((None, 1, D), lambda b, i, m: (b, 0, 0)),
         pl.BlockSpec((N_EXPERTS, D), lambda b, i, m: (0, 0)),
         pl.BlockSpec((N_EXPERTS, 1), lambda b, i, m: (0, 0))],
        [hf_spec,
         pl.BlockSpec((None, TOP_K, ts), lambda b, i, m: (b, 0, i)),
         pl.BlockSpec((None, TOP_K, ts), lambda b, i, m: (b, 0, i))],
        [jax.ShapeDtypeStruct(hf_shape, hf_dtype),
         jax.ShapeDtypeStruct((B, TOP_K, S), I32),
         jax.ShapeDtypeStruct((B, TOP_K, S), F32)],
        sem=("parallel", "parallel"), name="norm_route")(
            meta, x3, g_all, sc, sh, wr_t, b_router.reshape(N_EXPERTS, 1))
    T = B * S
    eidx = jnp.swapaxes(eidx, 1, 2).reshape(T, TOP_K)
    wts = jnp.swapaxes(wts, 1, 2).reshape(T, TOP_K)
    return hf.reshape((T,) + hf_shape[2:]), eidx, wts


def _expert_body(meta, x_ref, rw_ref, wg_ref, wu_ref, wd_ref, o_ref, *scratch, precise):
    blk = pl.program_id(0)
    n_used = meta[1]
    e = meta[2 + blk]
    e_prev = meta[2 + jnp.maximum(blk - 1, 0)]
    refs = (wg_ref, wu_ref, wd_ref)

    @pl.when((blk == 0) | (e != e_prev))
    def _():
        for i, r in enumerate(refs):
            if precise:
                hi, lo = _split(r[...])
                scratch[2 * i][...] = hi
                scratch[2 * i + 1][...] = lo
            else:
                scratch[i][...] = r[...].astype(BF16)

    @pl.when(blk < n_used)
    def _():
        if precise:
            x_hi, x_lo = _split(x_ref[...])
            h1 = _dot3(x_hi, x_lo, scratch[0][...], scratch[1][...])
            h2 = _dot3(x_hi, x_lo, scratch[2][...], scratch[3][...])
            a_hi, a_lo = _split(h1 * _sigmoid(h1) * h2)
            y = _dot3(a_hi, a_lo, scratch[4][...], scratch[5][...])
        else:
            x = x_ref[...]
            h1 = _dot(x, scratch[0][...])
            h2 = _dot(x, scratch[1][...])
            y = _dot((h1 * _sigmoid(h1) * h2).astype(BF16), scratch[2][...])
        o_ref[...] = y * rw_ref[...]

    @pl.when(blk >= n_used)
    def _():
        o_ref[...] = jnp.zeros_like(o_ref)


def _expert_gather_body(meta, tokn_ref, tok0_ref, dstp_ref, wg_ref, wu_ref, wd_ref, hf_hbm, y_hbm,
                        wg_s, wu_s, wd_s, xbuf, ybuf, sem_in, sem_out, *, bm):
    blk = pl.program_id(0)
    n_used = meta[1]
    e = meta[2 + blk]
    e_prev = meta[2 + jnp.maximum(blk - 1, 0)]
    cur = lax.rem(blk, 2)
    oth = 1 - cur

    n_slabs = xbuf.shape[2]

    def gather(tok_ref, i, slot):
        return pltpu.make_async_copy(hf_hbm.at[pl.ds(tok_ref[0, i], 1)], xbuf.at[slot, pl.ds(i, 1)],
                                     sem_in.at[slot])

    def scatter(i, slot):
        return pltpu.make_async_copy(ybuf.at[slot, pl.ds(i, 1)], y_hbm.at[pl.ds(dstp_ref[0, i], 1)],
                                     sem_out.at[slot])

    @pl.when((blk == 0) | (e != e_prev))
    def _():
        wg_s[...] = wg_ref[...].astype(BF16)
        wu_s[...] = wu_ref[...].astype(BF16)
        wd_s[...] = wd_ref[...].astype(BF16)

    @pl.when(blk == 0)
    def _():
        ybuf[1] = jnp.zeros(ybuf.shape[1:], F32)
        for i in range(bm):
            gather(tok0_ref, i, 0).start(priority=i % 2)

    @pl.when(blk <= n_used)
    def _():
        for i in range(bm):
            gather(tokn_ref, i, cur).wait()

    @pl.when((blk >= 1) & (blk <= n_used))
    def _():
        for i in range(bm):
            scatter(i, cur).wait()

    @pl.when(blk < n_used)
    def _():
        x = jnp.concatenate([xbuf[cur, :, c, :] for c in range(n_slabs)], axis=1).astype(BF16)
        h1 = _dot(x, wg_s[...])
        for i in range(bm):
            gather(tokn_ref, i, oth).start(priority=i % 2)
        h2 = _dot(x, wu_s[...])
        for i in range(bm):
            scatter(i, oth).start(priority=i % 2)
        y = _dot((h1 * _sigmoid(h1) * h2).astype(BF16), wd_s[...])
        for c in range(n_slabs):
            ybuf[cur, :, c, :] = y[:, c * LANES:(c + 1) * LANES]

    @pl.when(blk == n_used)
    def _():
        for i in range(bm):
            scatter(i, oth).start(priority=i % 2)
        for i in range(bm):
            scatter(i, oth).wait()


def moe(layer, hf, eidx, wts, w_gate, w_up, w_down, bm, precise):
    T, D = hf.shape[0], D_MODEL
    slab = hf.shape[1:]
    A = T * TOP_K
    n_blocks = -(-A // bm) + N_EXPERTS
    rows = n_blocks * bm
    e_flat = eidx.reshape(A)
    onehot = (e_flat[:, None] == jnp.arange(N_EXPERTS, dtype=I32)[None, :]).astype(I32)
    csum = jnp.cumsum(onehot, axis=0)
    counts = csum[-1]
    rank = jnp.sum((csum - onehot) * onehot, axis=1)
    padded = (counts + bm - 1) // bm * bm
    pad_end = jnp.cumsum(padded)
    pad_start = pad_end - padded
    pos = jnp.sum(pad_start[None, :] * onehot, axis=1) + rank
    n_used = pad_end[-1] // bm
    blk_ids = jnp.arange(n_blocks, dtype=I32)
    blk_exp = jnp.minimum(jnp.sum((pad_end[None, :] <= (blk_ids * bm)[:, None]).astype(I32), axis=1), N_EXPERTS - 1)
    last_exp = blk_exp[jnp.maximum(n_used - 1, 0)]
    blk_exp = jnp.where(blk_ids < n_used, blk_exp, last_exp)
    meta = jnp.concatenate([layer.reshape(1), n_used.reshape(1).astype(I32), blk_exp])
    n_scr = 6 if precise else 3
    scr = []
    for shape in ((D, D_EXPERT), (D, D_EXPERT), (D_EXPERT, D)):
        scr += [pltpu.VMEM(shape, BF16)] * (n_scr // 3)
    a_ids = jnp.arange(A, dtype=I32)
    if not precise:
        y_rows = TOP_K * T + bm
        spare = TOP_K * T + jnp.arange(rows, dtype=I32) % bm
        base = jnp.stack([jnp.zeros((rows,), I32), spare], axis=1)
        vals = jnp.stack([a_ids // TOP_K, (a_ids % TOP_K) * T + a_ids // TOP_K], axis=1)
        tab = base.at[pos].set(vals)
        tok_tab = tab[:, 0].reshape(n_blocks, 1, bm)
        dst_tab = jnp.concatenate([spare[:bm], tab[:, 1]]).reshape(n_blocks + 1, 1, bm)
        smem = functools.partial(pl.BlockSpec, memory_space=pltpu.SMEM)
        return _pcall(
            functools.partial(_expert_gather_body, bm=bm), (n_blocks,),
            [smem((None, 1, bm), lambda b, m: (jnp.minimum(b + 1, n_blocks - 1), 0, 0)),
             smem((None, 1, bm), lambda b, m: (0, 0, 0)),
             smem((None, 1, bm), lambda b, m: (b, 0, 0)),
             pl.BlockSpec((None, None, D, D_EXPERT), lambda b, m: (m[0], m[2 + b], 0, 0)),
             pl.BlockSpec((None, None, D, D_EXPERT), lambda b, m: (m[0], m[2 + b], 0, 0)),
             pl.BlockSpec((None, None, D_EXPERT, D), lambda b, m: (m[0], m[2 + b], 0, 0)),
             pl.BlockSpec(memory_space=pl.ANY)],
            pl.BlockSpec(memory_space=pl.ANY),
            jax.ShapeDtypeStruct((y_rows,) + slab, F32),
            scratch=scr + [pltpu.VMEM((2, bm) + slab, F32), pltpu.VMEM((2, bm) + slab, F32),
                           pltpu.SemaphoreType.DMA((2,)), pltpu.SemaphoreType.DMA((2,))],
            sem=("arbitrary",), name="experts_gather")(
                meta, tok_tab, tok_tab, dst_tab, w_gate, w_up, w_down, hf)
    row_tok = jnp.full((rows,), T, I32).at[pos].set(a_ids // TOP_K)
    row_w = jnp.zeros((rows,), F32).at[pos].set(wts.reshape(A))
    xs = jnp.concatenate([hf, jnp.zeros((1, D), hf.dtype)], axis=0)[row_tok]
    ys = _pcall(
        functools.partial(_expert_body, precise=precise), (n_blocks,),
        [pl.BlockSpec((bm, D), lambda b, m: (b, 0)),
         pl.BlockSpec((bm, 1), lambda b, m: (b, 0)),
         pl.BlockSpec((None, None, D, D_EXPERT), lambda b, m: (m[0], m[2 + b], 0, 0)),
         pl.BlockSpec((None, None, D, D_EXPERT), lambda b, m: (m[0], m[2 + b], 0, 0)),
         pl.BlockSpec((None, None, D_EXPERT, D), lambda b, m: (m[0], m[2 + b], 0, 0))],
        pl.BlockSpec((bm, D), lambda b, m: (b, 0)),
        jax.ShapeDtypeStruct((rows, D), F32),
        scratch=scr, sem=("arbitrary",), name="experts")(
            meta, xs, row_w.reshape(rows, 1), w_gate, w_up, w_down)
    return ys[pos.reshape(T, TOP_K).T.reshape(A)].reshape(TOP_K, T, D)


def _moe_residual_body(meta, x_ref, y0_ref, y1_ref, gt_ref, o_ref):
    o_ref[...] = x_ref[...] + gt_ref[...] * (y0_ref[...] + y1_ref[...])


def _moe_residual_w_body(meta, x_ref, y0_ref, y1_ref, gt_ref, w_ref, o_ref):
    rc = 32

    def piece(i, carry):
        rs = pl.ds(pl.multiple_of(i * rc, rc), rc)
        w = w_ref[rs, :]
        w0, w1 = w[:, 0:1], w[:, 1:2]
        for c in range(y0_ref.shape[1]):
            cs = slice(c * LANES, (c + 1) * LANES)
            o_ref[rs, cs] = x_ref[rs, cs] + gt_ref[:, cs] * (y0_ref[rs, c, :] * w0 + y1_ref[rs, c, :] * w1)
        return carry

    lax.fori_loop(0, y0_ref.shape[0] // rc, piece, 0)


def moe_residual(x3, y2, gt, wts=None):
    B, S, D = x3.shape
    ts = _row_tile(S)
    meta = jnp.zeros((1,), I32)
    x_spec = pl.BlockSpec((None, ts, D), lambda b, i, m: (b, i, 0))
    gt_spec = pl.BlockSpec((None, 1, D), lambda b, i, m: (b, 0, 0))
    if wts is not None:
        per = S // ts
        body = _moe_residual_w_body
        slab = y2.shape[1:]
        in_specs = [x_spec,
                    pl.BlockSpec((ts,) + slab, lambda b, i, m: (b * per + i, 0, 0)),
                    pl.BlockSpec((ts,) + slab, lambda b, i, m: (B * per + b * per + i, 0, 0)),
                    gt_spec,
                    pl.BlockSpec((None, ts, TOP_K), lambda b, i, m: (b, i, 0))]
        args = (x3, y2, y2, gt, wts.reshape(B, S, TOP_K))
    else:
        y3 = y2.reshape(TOP_K, B, S, D)
        body = _moe_residual_body
        in_specs = [x_spec,
                    pl.BlockSpec((None, None, ts, D), lambda b, i, m: (0, b, i, 0)),
                    pl.BlockSpec((None, None, ts, D), lambda b, i, m: (1, b, i, 0)),
                    gt_spec]
        args = (x3, y3, y3, gt)
    return _pcall(
        body, (B, S // ts), in_specs, x_spec, jax.ShapeDtypeStruct((B, S, D), F32),
        sem=("parallel", "parallel"), name="moe_residual")(meta, *args)


def _rope_tables(pos):
    half = HEAD_DIM // 2
    inv = ROPE_THETA ** (-jnp.arange(half, dtype=F32) / half)
    ang = pos.astype(F32)[:, None] * inv[None, :]
    cos, sin = jnp.cos(ang), jnp.sin(ang)
    return jnp.concatenate([cos, cos], axis=1), jnp.concatenate([-sin, sin], axis=1)


def _layer(layer, x3, mod, params, rope, sample_state):
    (g_mix, g_ffn, w_in, w_gates, w_tail, b_igate, b_fgate, g_mlstm_out, w_att_out, w_mlstm_out,
     w_mix_out, wr_t, b_router, w_exp_gate, w_exp_up, w_exp_down) = params
    B, S, D = x3.shape
    T = B * S
    sample = sample_state is not None
    act = F32 if sample else BF16
    meta = layer.reshape(1)
    sh1, sc1, gt1, sh2, sc2, gt2 = [m.reshape(B, 1, D) for m in jnp.split(mod, 6, axis=-1)]

    wins = []
    if sample:
        h = norm_mod(meta, x3, g_mix, sc1, sh1, act).reshape(T, D)
        proj = linear(meta, h, w_in, 0, GATE_OFF, F32, True, rot=(rope[0], rope[1], 2 * ATT_WIDTH // 512))
        qkv, qkvm = proj[:, :3 * ATT_WIDTH], proj[:, 3 * ATT_WIDTH:]
        for g in range(len(ATT_GROUPS)):
            kg = qkv[:, ATT_WIDTH + g * ATT_OUT:ATT_WIDTH + (g + 1) * ATT_OUT]
            vg = qkv[:, 2 * ATT_WIDTH + g * ATT_OUT:2 * ATT_WIDTH + (g + 1) * ATT_OUT]
            wins.append(jnp.concatenate([kg, vg], axis=1).reshape(B, S, 2, ATT_HPG, HEAD_DIM))
    else:
        hs = norm_mod_streams(meta, x3, g_mix, sc1, sh1)
        h = hs[0]
        qkv_g = [linear(meta, hs[g], w_in, g * ATT_OUT, 3 * ATT_OUT, F32, False, rot=(rope[g][0], rope[g][1], 2),
                        col_step=len(ATT_GROUPS)) for g in range(len(ATT_GROUPS))]
        for g, (win, dil) in enumerate(ATT_GROUPS):
            keep = min(win, S)
            L = S // dil
            kv = qkv_g[g].reshape(B, dil, L, 3 * ATT_OUT)[:, :, L - keep // dil:, ATT_OUT:]
            wins.append(jnp.swapaxes(kv, 1, 2).reshape(B, keep, 2, ATT_HPG, HEAD_DIM))
    if not sample:
        qkvm = linear(meta, h, w_in, 3 * ATT_WIDTH, 3 * M_WIDTH, act, False)
    gates = linear(meta, h, w_gates, 0, LANES, F32, sample, tn=LANES)
    tail = linear(meta, h, w_tail, 0, M_WIDTH + 2 * D_MODEL, act, sample)

    if sample:
        caches, st_c, st_n, st_m = sample_state
        att = sample_attention(meta, qkv, caches)
        hm, c1, n1, m1 = mlstm_sample(meta, qkvm, gates, b_igate, b_fgate, tail, g_mlstm_out, st_c, st_n, st_m)
    else:
        outs, lses = zip(*[band_attention(qkv_g[g], B, S, g) for g in range(len(ATT_GROUPS))])
        att = attn_merge(outs, lses)
        hm, c1, n1, m1 = mlstm_prompt(meta, qkvm, gates, b_igate, b_fgate, tail, g_mlstm_out, B, S)

    merged = merge(meta, att, hm, w_att_out, w_mlstm_out, tail, act, sample)
    x2 = linear_residual(meta, merged, w_mix_out, x3.reshape(T, D), gt1, S, sample)
    x3 = x2.reshape(B, S, D)

    hf, eidx, wts = norm_route(meta, x3, g_ffn, sc2, sh2, wr_t, b_router, F32, slabs=not sample)
    bm = MOE_BLOCK_SAMPLE if sample else MOE_BLOCK_PROMPT
    y2 = moe(layer, hf, eidx, wts, w_exp_gate, w_exp_up, w_exp_down, bm, sample)
    x3 = moe_residual(x3, y2, gt2, None if sample else wts)
    return x3, (wins[0], wins[1], wins[2], c1, n1, m1)


def kernel(x_prompt, x_sample, cache_win_w128, cache_win_w512, cache_win_w2048, state_mlstm_c, state_mlstm_n,
           state_mlstm_m, c_prompt, c_sample, w_ada, b_ada, g_mix, g_ffn, w_in, b_igate, b_fgate, g_mlstm_out,
           w_att_out, w_mlstm_out, w_mix_out, w_router, b_router, w_exp_gate, w_exp_up, w_exp_down, g_final):
    B, S, D = x_prompt.shape
    Bd, Sd, _ = x_sample.shape
    assert Sd == 1 and D == D_MODEL and w_in.shape[2] == D_IN

    n_c = B + Bd
    c_rows = -(-n_c // SUBLANES) * SUBLANES
    c_all = jnp.pad(jnp.concatenate([c_prompt, c_sample], axis=0), ((0, c_rows - n_c), (0, 0)))
    mod = ada_modulation(c_all, w_ada, b_ada)

    w_gates = jnp.pad(w_in[:, :, GATE_OFF:TAIL_OFF], ((0, 0), (0, 0), (0, LANES - 2 * M_HEADS)))
    w_tail = w_in[:, :, TAIL_OFF:]
    params = (g_mix.reshape(DEPTH, 1, D), g_ffn.reshape(DEPTH, 1, D), w_in, w_gates, w_tail, b_igate, b_fgate,
              g_mlstm_out, w_att_out, w_mlstm_out, w_mix_out, w_router.T, b_router,
              w_exp_gate, w_exp_up, w_exp_down)
    rope_p = [_rope_tables((jnp.arange(S // dil, dtype=I32)[None, :] * dil
                            + jnp.arange(dil, dtype=I32)[:, None]).reshape(S)) for _, dil in ATT_GROUPS]
    rope_s = _rope_tables(jnp.full((Bd,), PAST_LEN, I32))
    caches = []
    for c, (win, dil) in zip((cache_win_w128, cache_win_w512, cache_win_w2048), ATT_GROUPS):
        assert c.shape[2] == win and win == ATT_BLOCK * dil
        caches.append(c[:, :, ::dil].reshape(DEPTH, Bd, ATT_BLOCK, 2 * ATT_OUT))
    sample_state = (caches, state_mlstm_c, state_mlstm_n, state_mlstm_m)

    def body(carry, layer):
        xp, xs = carry
        mod_l = lax.dynamic_index_in_dim(mod, layer, 0, keepdims=False)
        xp, out_p = _layer(layer, xp, mod_l[:B], params, rope_p, None)
        xs, out_s = _layer(layer, xs, mod_l[B:n_c], params, rope_s, sample_state)
        return (xp, xs), (out_p, out_s)

    (xp, xs), (out_p, out_s) = lax.scan(body, (x_prompt, x_sample), jnp.arange(DEPTH, dtype=I32))
    y_prompt = norm_final(xp, g_final)
    y_sample = norm_final(xs, g_final)
    return (y_prompt, y_sample) + tuple(out_p) + tuple(out_s)
```

```python
import functools

import numpy as np
import jax
import jax.numpy as jnp
from jax import lax
from jax.experimental import pallas as pl
from jax.experimental.pallas import tpu as pltpu

F32 = jnp.float32
BF16 = jnp.bfloat16
I32 = jnp.int32

D_MODEL = 2048
DEPTH = 4
PAST_LEN = 16384
HEAD_DIM = 128
ATT_GROUPS = ((128, 1), (512, 4), (2048, 16))
ATT_HPG = 4
ATT_WIDTH = len(ATT_GROUPS) * ATT_HPG * HEAD_DIM
ATT_OUT = ATT_HPG * HEAD_DIM
ATT_BLOCK = 128
ROPE_THETA = 10000.0
M_HEADS = 4
M_WIDTH = D_MODEL // 2
M_HD = M_WIDTH // M_HEADS
N_EXPERTS = 16
N_EGROUPS = 4
EPG = N_EXPERTS // N_EGROUPS
TOP_K = 2
D_EXPERT = D_MODEL // 4
EPS = 1e-6
NEG_INF = -1e30
D_IN = 3 * ATT_WIDTH + 3 * M_WIDTH + 2 * M_HEADS + M_WIDTH + 2 * D_MODEL
GATE_OFF = 3 * ATT_WIDTH + 3 * M_WIDTH
TAIL_OFF = GATE_OFF + 2 * M_HEADS

V7X_VMEM_BYTES = 64 * 1024 * 1024
LANES = 128
SUBLANES = 8
VMEM_LIMIT = 56 * 1024 * 1024

MLSTM_CHUNK = 256
MOE_BLOCK_PROMPT = 256
MOE_BLOCK_SAMPLE = 8


def _pcall(body, grid, in_specs, out_specs, out_shape, scratch=(), sem=None, name=None):
    return pl.pallas_call(
        body,
        grid_spec=pltpu.PrefetchScalarGridSpec(
            num_scalar_prefetch=1, grid=grid, in_specs=in_specs, out_specs=out_specs,
            scratch_shapes=list(scratch)),
        out_shape=out_shape,
        compiler_params=pltpu.CompilerParams(dimension_semantics=sem, vmem_limit_bytes=VMEM_LIMIT),
        name=name)


def _split(x):
    hi = x.astype(BF16)
    lo = (x - hi.astype(F32)).astype(BF16)
    return hi, lo


def _dot(a, b):
    return jnp.dot(a, b, preferred_element_type=F32)


def _dot_nt(a, b):
    return lax.dot_general(a, b, (((1,), (1,)), ((), ())), preferred_element_type=F32)


def _dot_tn(a, b):
    return lax.dot_general(a, b, (((0,), (0,)), ((), ())), preferred_element_type=F32)


def _dot3(a_hi, a_lo, w_hi, w_lo):
    return _dot(a_hi, w_hi) + (_dot(a_hi, w_lo) + _dot(a_lo, w_hi))


def _sigmoid(x):
    return 1.0 / (1.0 + jnp.exp(-x))


def _log_sigmoid(x):
    return jnp.minimum(x, 0.0) - jnp.log1p(jnp.exp(-jnp.abs(x)))


def _ada_body(meta, c_ref, w_ref, b_ref, o_ref):
    c = c_ref[...]
    a_hi, a_lo = _split(c * _sigmoid(c))
    w_hi, w_lo = _split(w_ref[...])
    o_ref[...] = _dot3(a_hi, a_lo, w_hi, w_lo) + b_ref[...]


def ada_modulation(c_all, w_ada, b_ada):
    rows = c_all.shape[0]
    n_out = w_ada.shape[2]
    tn = 1024
    meta = jnp.zeros((1,), I32)
    return _pcall(
        _ada_body, (DEPTH, n_out // tn),
        [pl.BlockSpec((rows, D_MODEL), lambda l, j, m: (0, 0)),
         pl.BlockSpec((None, D_MODEL, tn), lambda l, j, m: (l, 0, j)),
         pl.BlockSpec((None, 1, tn), lambda l, j, m: (l, 0, j))],
        pl.BlockSpec((None, rows, tn), lambda l, j, m: (l, 0, j)),
        jax.ShapeDtypeStruct((DEPTH, rows, n_out), F32),
        sem=("parallel", "parallel"), name="ada_mod")(meta, c_all, w_ada, b_ada.reshape(DEPTH, 1, n_out))


def _rms(x, g):
    return x * lax.rsqrt(jnp.mean(x * x, axis=-1, keepdims=True) + EPS) * g


def _norm_mod_body(meta, x_ref, g_ref, sc_ref, sh_ref, o_ref):
    y = _rms(x_ref[...], g_ref[...])
    o_ref[...] = (y * (1.0 + sc_ref[...]) + sh_ref[...]).astype(o_ref.dtype)


def _norm_plain_body(meta, x_ref, g_ref, o_ref):
    o_ref[...] = _rms(x_ref[...], g_ref[...]).astype(o_ref.dtype)


def _row_tile(S):
    return min(S, 512)


def norm_mod(meta, x3, g_all, sc, sh, out_dtype):
    B, S, D = x3.shape
    ts = _row_tile(S)
    return _pcall(
        _norm_mod_body, (B, S // ts),
        [pl.BlockSpec((None, ts, D), lambda b, i, m: (b, i, 0)),
         pl.BlockSpec((None, 1, D), lambda b, i, m: (m[0], 0, 0)),
         pl.BlockSpec((None, 1, D), lambda b, i, m: (b, 0, 0)),
         pl.BlockSpec((None, 1, D), lambda b, i, m: (b, 0, 0))],
        pl.BlockSpec((None, ts, D), lambda b, i, m: (b, i, 0)),
        jax.ShapeDtypeStruct((B, S, D), out_dtype),
        sem=("parallel", "parallel"), name="norm_mod")(meta, x3, g_all, sc, sh)


def _norm_streams_body(meta, x_ref, g_ref, sc_ref, sh_ref, o0_ref, o1_ref, o2_ref, y_s, *, ts):
    y = _rms(x_ref[...], g_ref[...]) * (1.0 + sc_ref[...]) + sh_ref[...]
    o0_ref[...] = y.astype(o0_ref.dtype)
    n_slabs = y.shape[1] // LANES
    for c in range(n_slabs):
        y_s[c] = y[:, c * LANES:(c + 1) * LANES]
    for o_ref, (_, dil) in ((o1_ref, ATT_GROUPS[1]), (o2_ref, ATT_GROUPS[2])):
        for r in range(dil):
            for c in range(n_slabs):
                o_ref[r, :, c * LANES:(c + 1) * LANES] = (
                    y_s[c, pl.ds(r, ts // dil, stride=dil), :].astype(o_ref.dtype))


def norm_mod_streams(meta, x3, g_all, sc, sh):
    B, S, D = x3.shape
    ts = _row_tile(S)
    d1, d2 = ATT_GROUPS[1][1], ATT_GROUPS[2][1]
    outs = _pcall(
        functools.partial(_norm_streams_body, ts=ts), (B, S // ts),
        [pl.BlockSpec((None, ts, D), lambda b, i, m: (b, i, 0)),
         pl.BlockSpec((None, 1, D), lambda b, i, m: (m[0], 0, 0)),
         pl.BlockSpec((None, 1, D), lambda b, i, m: (b, 0, 0)),
         pl.BlockSpec((None, 1, D), lambda b, i, m: (b, 0, 0))],
        [pl.BlockSpec((None, ts, D), lambda b, i, m: (b, i, 0)),
         pl.BlockSpec((None, d1, ts // d1, D), lambda b, i, m: (b, 0, i, 0)),
         pl.BlockSpec((None, d2, ts // d2, D), lambda b, i, m: (b, 0, i, 0))],
        [jax.ShapeDtypeStruct((B, S, D), BF16),
         jax.ShapeDtypeStruct((B, d1, S // d1, D), BF16),
         jax.ShapeDtypeStruct((B, d2, S // d2, D), BF16)],
        scratch=[pltpu.VMEM((D // LANES, ts, LANES), F32)],
        sem=("parallel", "parallel"), name="norm_mod_streams")(meta, x3, g_all, sc, sh)
    return [o.reshape(B * S, D) for o in outs]


def norm_final(x3, g):
    B, S, D = x3.shape
    ts = _row_tile(S)
    meta = jnp.zeros((1,), I32)
    return _pcall(
        _norm_plain_body, (B, S // ts),
        [pl.BlockSpec((None, ts, D), lambda b, i, m: (b, i, 0)),
         pl.BlockSpec((1, D), lambda b, i, m: (0, 0))],
        pl.BlockSpec((None, ts, D), lambda b, i, m: (b, i, 0)),
        jax.ShapeDtypeStruct((B, S, D), F32),
        sem=("parallel", "parallel"), name="norm_final")(meta, x3, g.reshape(1, D))


def _matmul_tile(a_ref, w_ref, precise):
    w = w_ref[...]
    if precise:
        a_hi, a_lo = _split(a_ref[...])
        w_hi, w_lo = _split(w)
        return _dot3(a_hi, a_lo, w_hi, w_lo)
    return _dot(a_ref[...], w.astype(BF16))


def _linear_body(meta, a_ref, w_ref, o_ref, *, precise):
    o_ref[...] = _matmul_tile(a_ref, w_ref, precise).astype(o_ref.dtype)


def _linear_rot_body(meta, a_ref, w_ref, cos_ref, sin_ref, o_ref, *, precise, n_rot, tn):
    acc = _matmul_tile(a_ref, w_ref, precise)
    j = pl.program_id(1)

    @pl.when(j < n_rot)
    def _():
        c = cos_ref[...]
        s = sin_ref[...]
        for h in range(tn // HEAD_DIM):
            hs = slice(h * HEAD_DIM, (h + 1) * HEAD_DIM)
            x = acc[:, hs]
            o_ref[:, hs] = (x * c + pltpu.roll(x, HEAD_DIM // 2, 1) * s).astype(o_ref.dtype)

    @pl.when(j >= n_rot)
    def _():
        o_ref[...] = acc.astype(o_ref.dtype)


def _linear_res_body(meta, a_ref, w_ref, x_ref, gt_ref, o_ref, *, precise):
    o_ref[...] = x_ref[...] + gt_ref[...] * _matmul_tile(a_ref, w_ref, precise)


def _m_tile(T):
    return min(T, 2048)


def linear(meta, a, w, col_off, n_out, out_dtype, precise, tn=512, rot=None, col_step=1):
    T, K = a.shape
    tm = _m_tile(T)
    off = col_off // tn
    in_specs = [pl.BlockSpec((tm, K), lambda i, j, m: (i, 0)),
                pl.BlockSpec((None, K, tn), lambda i, j, m: (m[0], 0, off + j * col_step))]
    args = [a, w]
    if rot is None:
        body = functools.partial(_linear_body, precise=precise)
    else:
        cos_t, sin_t, n_rot = rot
        nper = cos_t.shape[0] // tm
        in_specs += [pl.BlockSpec((tm, HEAD_DIM), lambda i, j, m: (i % nper, 0)),
                     pl.BlockSpec((tm, HEAD_DIM), lambda i, j, m: (i % nper, 0))]
        args += [cos_t, sin_t]
        body = functools.partial(_linear_rot_body, precise=precise, n_rot=n_rot, tn=tn)
    return _pcall(
        body, (T // tm, n_out // tn), in_specs,
        pl.BlockSpec((tm, tn), lambda i, j, m: (i, j)),
        jax.ShapeDtypeStruct((T, n_out), out_dtype),
        sem=("parallel", "arbitrary"), name="linear")(meta, *args)


def linear_residual(meta, a, w, x2, gt, rows_per_batch, precise, tn=512):
    T, K = a.shape
    D = x2.shape[1]
    tm = _m_tile(T)
    if rows_per_batch == 1:
        gt_arr = gt.reshape(T, D)
        gt_spec = pl.BlockSpec((tm, tn), lambda i, j, m: (i, j))
    else:
        per = rows_per_batch // tm
        gt_arr = gt
        gt_spec = pl.BlockSpec((None, 1, tn), lambda i, j, m: (i // per, 0, j))
    return _pcall(
        functools.partial(_linear_res_body, precise=precise), (T // tm, D // tn),
        [pl.BlockSpec((tm, K), lambda i, j, m: (i, 0)),
         pl.BlockSpec((None, K, tn), lambda i, j, m: (m[0], 0, j)),
         pl.BlockSpec((tm, tn), lambda i, j, m: (i, j)),
         gt_spec],
        pl.BlockSpec((tm, tn), lambda i, j, m: (i, j)),
        jax.ShapeDtypeStruct((T, D), F32),
        sem=("parallel", "arbitrary"), name="linear_residual")(meta, a, w, x2, gt_arr)


def _attn_body(meta, q_ref, k_ref, v_ref, o_ref, lse_ref, *, nb):
    scale = HEAD_DIM ** -0.5
    qi = lax.broadcasted_iota(I32, (ATT_BLOCK, ATT_BLOCK), 0)
    kj = lax.broadcasted_iota(I32, (ATT_BLOCK, ATT_BLOCK), 1)
    m_cur = kj <= qi
    m_prev = qi <= kj

    def blk(n, carry):
        qs = pl.multiple_of(n * ATT_BLOCK, ATT_BLOCK)
        ps = pl.multiple_of(jnp.maximum(n - 1, 0) * ATT_BLOCK, ATT_BLOCK)
        prev_cap = jnp.where(n > 0, -NEG_INF, NEG_INF)
        for h in range(ATT_HPG):
            hs = slice(h * HEAD_DIM, (h + 1) * HEAD_DIM)
            q = q_ref[pl.ds(qs, ATT_BLOCK), hs].astype(BF16)
            kc = k_ref[pl.ds(qs, ATT_BLOCK), hs].astype(BF16)
            kp = k_ref[pl.ds(ps, ATT_BLOCK), hs].astype(BF16)
            vc = v_ref[pl.ds(qs, ATT_BLOCK), hs].astype(BF16)
            vp = v_ref[pl.ds(ps, ATT_BLOCK), hs].astype(BF16)
            sc = jnp.where(m_cur, _dot_nt(q, kc) * scale, NEG_INF)
            sp = jnp.minimum(jnp.where(m_prev, _dot_nt(q, kp) * scale, NEG_INF), prev_cap)
            mx = jnp.maximum(jnp.max(sc, axis=1, keepdims=True), jnp.max(sp, axis=1, keepdims=True))
            pc = jnp.exp(sc - mx)
            pp = jnp.exp(sp - mx)
            den = jnp.sum(pc, axis=1, keepdims=True) + jnp.sum(pp, axis=1, keepdims=True)
            o = (_dot(pc.astype(BF16), vc) + _dot(pp.astype(BF16), vp)) / den
            o_ref[pl.ds(qs, ATT_BLOCK), hs] = o.astype(o_ref.dtype)
            lse_ref[pl.ds(qs, ATT_BLOCK), hs] = jnp.broadcast_to(mx + jnp.log(den), (ATT_BLOCK, HEAD_DIM))
        return carry

    lax.fori_loop(0, nb, blk, 0, unroll=min(nb, 2))


def band_attention(qkv_g, B, S, g):
    _, dil = ATT_GROUPS[g]
    L = S // dil
    assert L % ATT_BLOCK == 0
    qv = qkv_g.reshape(B, dil, L, 3 * ATT_OUT)
    meta = jnp.zeros((1,), I32)
    o, lse = _pcall(
        functools.partial(_attn_body, nb=L // ATT_BLOCK), (B, dil),
        [pl.BlockSpec((None, None, L, ATT_OUT), lambda b, r, m: (b, r, 0, 0)),
         pl.BlockSpec((None, None, L, ATT_OUT), lambda b, r, m: (b, r, 0, 1)),
         pl.BlockSpec((None, None, L, ATT_OUT), lambda b, r, m: (b, r, 0, 2))],
        [pl.BlockSpec((None, L, ATT_OUT), lambda b, r, m: (b, 0, r)),
         pl.BlockSpec((None, L, ATT_OUT), lambda b, r, m: (b, 0, r))],
        [jax.ShapeDtypeStruct((B, L, dil * ATT_OUT), BF16),
         jax.ShapeDtypeStruct((B, L, dil * ATT_OUT), F32)],
        sem=("parallel", "parallel"), name="band_attention")(meta, qv, qv, qv)
    return o.reshape(B * S, ATT_OUT), lse.reshape(B * S, ATT_OUT)


def _attn_merge_body(meta, o0, o1, o2, l0, l1, l2, att_ref):
    a0, a1, a2 = l0[...], l1[...], l2[...]
    mx = jnp.maximum(jnp.maximum(a0, a1), a2)
    e0, e1, e2 = jnp.exp(a0 - mx), jnp.exp(a1 - mx), jnp.exp(a2 - mx)
    num = e0 * o0[...].astype(F32) + e1 * o1[...].astype(F32) + e2 * o2[...].astype(F32)
    att_ref[...] = (num / (e0 + e1 + e2)).astype(att_ref.dtype)


def attn_merge(outs, lses):
    T = outs[0].shape[0]
    tm = _m_tile(T)
    spec = pl.BlockSpec((tm, ATT_OUT), lambda i, m: (i, 0))
    meta = jnp.zeros((1,), I32)
    return _pcall(_attn_merge_body, (T // tm,), [spec] * 6, spec,
                  jax.ShapeDtypeStruct((T, ATT_OUT), BF16),
                  sem=("parallel",), name="attn_merge")(meta, *outs, *lses)


def _attn_sample_body(meta, q_ref, c0_ref, c1_ref, c2_ref, att_ref):
    scale = HEAD_DIM ** -0.5
    row = q_ref[...]
    caches = (c0_ref, c1_ref, c2_ref)
    for h in range(ATT_HPG):
        outs, lses = [], []
        for g in range(len(ATT_GROUPS)):
            col = (g * ATT_HPG + h) * HEAD_DIM
            q = row[:, col:col + HEAD_DIM]
            k_new = row[:, ATT_WIDTH + col:ATT_WIDTH + col + HEAD_DIM]
            v_new = row[:, 2 * ATT_WIDTH + col:2 * ATT_WIDTH + col + HEAD_DIM]
            kc = caches[g][:, h * HEAD_DIM:(h + 1) * HEAD_DIM]
            vc = caches[g][:, ATT_OUT + h * HEAD_DIM:ATT_OUT + (h + 1) * HEAD_DIM]
            s = jnp.sum(kc * q, axis=1, keepdims=True) * scale
            s0 = jnp.sum(k_new * q, axis=1, keepdims=True) * scale
            mx = jnp.maximum(jnp.max(s, axis=0, keepdims=True), s0)
            p = jnp.exp(s - mx)
            p0 = jnp.exp(s0 - mx)
            den = jnp.sum(p, axis=0, keepdims=True) + p0
            outs.append((jnp.sum(vc * p, axis=0, keepdims=True) + p0 * v_new) / den)
            lses.append(mx + jnp.log(den))
        mxl = jnp.maximum(jnp.maximum(lses[0], lses[1]), lses[2])
        es = [jnp.exp(l - mxl) for l in lses]
        num = es[0] * outs[0] + es[1] * outs[1] + es[2] * outs[2]
        att_ref[:, h * HEAD_DIM:(h + 1) * HEAD_DIM] = num / (es[0] + es[1] + es[2])


def sample_attention(meta, qkv, caches):
    Bd = qkv.shape[0]
    wcols = 3 * ATT_WIDTH
    in_specs = [pl.BlockSpec((None, 1, wcols), lambda b, m: (b, 0, 0))]
    args = [qkv.reshape(Bd, 1, wcols)]
    for g in range(len(ATT_GROUPS)):
        args.append(caches[g])
        in_specs.append(pl.BlockSpec((None, None, ATT_BLOCK, 2 * ATT_OUT), lambda b, m: (m[0], b, 0, 0)))
    att = _pcall(
        _attn_sample_body, (Bd,), in_specs,
        pl.BlockSpec((None, 1, ATT_OUT), lambda b, m: (b, 0, 0)),
        jax.ShapeDtypeStruct((Bd, 1, ATT_OUT), F32),
        sem=("parallel",), name="sample_attention")(meta, *args)
    return att.reshape(Bd, ATT_OUT)


def _mlstm_body(meta, q_ref, k_ref, v_ref, gc_ref, gr_ref, bc_ref, br_ref, og_ref, g_ref,
                hm_ref, caug_ref, m_ref, c_s, m_s, *, Lc, nc):
    h = pl.program_id(1)
    c = pl.program_id(2)

    @pl.when(c == 0)
    def _():
        c_s[...] = jnp.zeros_like(c_s)
        m_s[...] = jnp.zeros_like(m_s)

    gcol = gc_ref[...] + bc_ref[...]
    lane = lax.broadcasted_iota(I32, gcol.shape, 1)
    i_col = jnp.sum(jnp.where(lane == h, gcol, 0.0), axis=1, keepdims=True)
    f_col = jnp.sum(jnp.where(lane == h + M_HEADS, gcol, 0.0), axis=1, keepdims=True)
    grow = gr_ref[...] + br_ref[...]
    sub = lax.broadcasted_iota(I32, grow.shape, 0)
    i_row = jnp.sum(jnp.where(sub == h, grow, 0.0), axis=0, keepdims=True)
    f_row = jnp.sum(jnp.where(sub == h + M_HEADS, grow, 0.0), axis=0, keepdims=True)
    lf_col = _log_sigmoid(f_col)
    lf_row = _log_sigmoid(f_row)

    t_i = lax.broadcasted_iota(I32, (Lc, Lc), 0)
    s_i = lax.broadcasted_iota(I32, (Lc, Lc), 1)
    causal = s_i <= t_i
    b_col = jnp.sum(jnp.where(causal, lf_row, 0.0), axis=1, keepdims=True)
    b_row = jnp.sum(jnp.where(t_i <= s_i, lf_col, 0.0), axis=0, keepdims=True)
    a_row = i_row - b_row
    cmax_col = jnp.max(jnp.where(causal, a_row, -jnp.inf), axis=1, keepdims=True)
    m_prev = m_s[0:1, 0:1]
    m_t = b_col + jnp.maximum(m_prev, cmax_col)
    inter = jnp.exp(m_prev + b_col - m_t)
    dmat = jnp.exp(jnp.where(causal, a_row + (b_col - m_t), NEG_INF))

    q = q_ref[...]
    k = k_ref[...]
    v = v_ref[...]
    kscale = M_HD ** -0.5
    one_col = jnp.where(lax.broadcasted_iota(I32, (Lc, LANES), 1) == 0, 1.0, 0.0).astype(BF16)
    v_aug = jnp.concatenate([v, one_col], axis=1)
    w = _dot_nt(q, k) * kscale * dmat
    caug = c_s[...]
    num = inter * _dot(q, caug.astype(BF16)) + _dot(w.astype(BF16), v_aug)
    den = num[:, M_HD:M_HD + 1]
    hh = num[:, :M_HD] / jnp.maximum(jnp.abs(den), jnp.exp(-m_t))

    b_last = jnp.sum(lf_row, axis=1, keepdims=True)
    m_last = b_last + jnp.maximum(m_prev, jnp.max(a_row, axis=1, keepdims=True))
    decay = jnp.exp(m_prev + b_last - m_last)
    wk_col = jnp.exp(i_col + b_last - b_col - m_last)
    kw = (k.astype(F32) * (wk_col * kscale)).astype(BF16)
    c_new = decay * caug + _dot_tn(kw, v_aug)
    c_s[...] = c_new
    m_s[...] = jnp.broadcast_to(m_last, m_s.shape)

    y = _rms(hh, g_ref[...]) * _sigmoid(og_ref[...].astype(F32))
    hm_ref[...] = y.astype(hm_ref.dtype)

    @pl.when(c == nc - 1)
    def _():
        caug_ref[...] = c_new
        m_ref[...] = jnp.broadcast_to(m_last, m_ref.shape)


def mlstm_prompt(meta, qkvm, gates, b_i, b_f, tail, g_out, B, S):
    Lc = min(MLSTM_CHUNK, S)
    nc = S // Lc
    caw = M_HD + LANES
    q3 = qkvm.reshape(B, S, 3 * M_WIDTH)
    g3 = gates.reshape(B, S, LANES)
    gr = jnp.swapaxes(g3[:, :, :SUBLANES], 1, 2)
    bias = jnp.concatenate([b_i, b_f], axis=1)
    bc = jnp.pad(bias, ((0, 0), (0, LANES - 2 * M_HEADS))).reshape(DEPTH, 1, LANES)
    br = bias.reshape(DEPTH, 2 * M_HEADS, 1)
    t3 = tail.reshape(B, S, tail.shape[1])
    hm, caug, mfin = _pcall(
        functools.partial(_mlstm_body, Lc=Lc, nc=nc), (B, M_HEADS, nc),
        [pl.BlockSpec((None, Lc, M_HD), lambda b, h, c, m: (b, c, h)),
         pl.BlockSpec((None, Lc, M_HD), lambda b, h, c, m: (b, c, M_HEADS + h)),
         pl.BlockSpec((None, Lc, M_HD), lambda b, h, c, m: (b, c, 2 * M_HEADS + h)),
         pl.BlockSpec((None, Lc, LANES), lambda b, h, c, m: (b, c, 0)),
         pl.BlockSpec((None, SUBLANES, Lc), lambda b, h, c, m: (b, 0, c)),
         pl.BlockSpec((None, 1, LANES), lambda b, h, c, m: (m[0], 0, 0)),
         pl.BlockSpec((None, 2 * M_HEADS, 1), lambda b, h, c, m: (m[0], 0, 0)),
         pl.BlockSpec((None, Lc, M_HD), lambda b, h, c, m: (b, c, h)),
         pl.BlockSpec((None, 1, M_HD), lambda b, h, c, m: (m[0], 0, h))],
        [pl.BlockSpec((None, Lc, M_HD), lambda b, h, c, m: (b, c, h)),
         pl.BlockSpec((None, None, M_HD, caw), lambda b, h, c, m: (b, h, 0, 0)),
         pl.BlockSpec((None, None, SUBLANES, LANES), lambda b, h, c, m: (b, h, 0, 0))],
        [jax.ShapeDtypeStruct((B, S, M_WIDTH), BF16),
         jax.ShapeDtypeStruct((B, M_HEADS, M_HD, caw), F32),
         jax.ShapeDtypeStruct((B, M_HEADS, SUBLANES, LANES), F32)],
        scratch=[pltpu.VMEM((M_HD, caw), F32), pltpu.VMEM((SUBLANES, LANES), F32)],
        sem=("parallel", "parallel", "arbitrary"), name="mlstm_prompt")(
            meta, q3, q3, q3, g3, gr, bc, br, t3, g_out.reshape(DEPTH, 1, M_WIDTH))
    return (hm.reshape(B * S, M_WIDTH), caug[..., :M_HD], caug[..., M_HD], mfin[:, :, 0, 0])


def _mlstm_sample_body(meta, x_ref, g_ref, bc_ref, og_ref, gout_ref, c0_ref, n0_ref, m0_ref,
                       hm_ref, c1_ref, n1_ref, m1_ref):
    row = x_ref[...]
    gates = g_ref[...] + bc_ref[...]
    eye = lax.broadcasted_iota(I32, (M_HD, M_HD), 0) == lax.broadcasted_iota(I32, (M_HD, M_HD), 1)
    kscale = M_HD ** -0.5
    for h in range(M_HEADS):
        hs = slice(h * M_HD, (h + 1) * M_HD)
        q = row[:, h * M_HD:(h + 1) * M_HD]
        k = row[:, M_WIDTH + h * M_HD:M_WIDTH + (h + 1) * M_HD] * kscale
        v = row[:, 2 * M_WIDTH + h * M_HD:2 * M_WIDTH + (h + 1) * M_HD]
        q_col = jnp.sum(jnp.where(eye, q, 0.0), axis=1, keepdims=True)
        k_col = jnp.sum(jnp.where(eye, k, 0.0), axis=1, keepdims=True)
        ii = gates[:, h:h + 1]
        lf = _log_sigmoid(gates[:, M_HEADS + h:M_HEADS + h + 1])
        C = c0_ref[h]
        n = n0_ref[h:h + 1, :]
        m = m0_ref[:, h:h + 1]
        a = ii - lf
        m_t = lf + jnp.maximum(m, a)
        inter = jnp.exp(m + lf - m_t)
        dm = jnp.exp(a + (lf - m_t))
        w = jnp.sum(q * k, axis=1, keepdims=True) * dm
        num = inter * jnp.sum(C * q_col, axis=0, keepdims=True) + w * v
        den = inter * jnp.sum(q * n, axis=1, keepdims=True) + w
        hh = num / jnp.maximum(jnp.abs(den), jnp.exp(-m_t))
        wk = jnp.exp(ii + lf - lf - m_t)
        c1_ref[h] = inter * C + (wk * k_col) * v
        n1_ref[h:h + 1, :] = inter * n + wk * k
        m1_ref[:, h:h + 1] = m_t
        y = _rms(hh, gout_ref[:, hs]) * _sigmoid(og_ref[:, hs])
        hm_ref[:, hs] = y


def mlstm_sample(meta, qkvm, gates, b_i, b_f, tail, g_out, st_c, st_n, st_m):
    Bd = qkvm.shape[0]
    bias = jnp.concatenate([b_i, b_f], axis=1)
    bc = jnp.pad(bias, ((0, 0), (0, LANES - 2 * M_HEADS))).reshape(DEPTH, 1, LANES)
    tw = tail.shape[1]
    hm, c1, n1, m1 = _pcall(
        _mlstm_sample_body, (Bd,),
        [pl.BlockSpec((None, 1, 3 * M_WIDTH), lambda b, m: (b, 0, 0)),
         pl.BlockSpec((None, 1, LANES), lambda b, m: (b, 0, 0)),
         pl.BlockSpec((None, 1, LANES), lambda b, m: (m[0], 0, 0)),
         pl.BlockSpec((None, 1, M_WIDTH), lambda b, m: (b, 0, 0)),
         pl.BlockSpec((None, 1, M_WIDTH), lambda b, m: (m[0], 0, 0)),
         pl.BlockSpec((None, None, M_HEADS, M_HD, M_HD), lambda b, m: (m[0], b, 0, 0, 0)),
         pl.BlockSpec((None, None, M_HEADS, M_HD), lambda b, m: (m[0], b, 0, 0)),
         pl.BlockSpec((None, None, 1, M_HEADS), lambda b, m: (m[0], b, 0, 0))],
        [pl.BlockSpec((None, 1, M_WIDTH), lambda b, m: (b, 0, 0)),
         pl.BlockSpec((None, M_HEADS, M_HD, M_HD), lambda b, m: (b, 0, 0, 0)),
         pl.BlockSpec((None, M_HEADS, M_HD), lambda b, m: (b, 0, 0)),
         pl.BlockSpec((None, 1, M_HEADS), lambda b, m: (b, 0, 0))],
        [jax.ShapeDtypeStruct((Bd, 1, M_WIDTH), F32),
         jax.ShapeDtypeStruct((Bd, M_HEADS, M_HD, M_HD), F32),
         jax.ShapeDtypeStruct((Bd, M_HEADS, M_HD), F32),
         jax.ShapeDtypeStruct((Bd, 1, M_HEADS), F32)],
        sem=("parallel",), name="mlstm_sample")(
            meta, qkvm.reshape(Bd, 1, 3 * M_WIDTH), gates.reshape(Bd, 1, LANES), bc,
            tail.reshape(Bd, 1, tw), g_out.reshape(DEPTH, 1, M_WIDTH),
            st_c, st_n, st_m.reshape(DEPTH, Bd, 1, M_HEADS))
    return hm.reshape(Bd, M_WIDTH), c1, n1, m1.reshape(Bd, M_HEADS)


def _merge_body(meta, att_ref, hm_ref, wa_ref, wm_ref, ga_ref, gb_ref, o_ref, *, precise):
    ya = _matmul_tile(att_ref, wa_ref, precise)
    ym = _matmul_tile(hm_ref, wm_ref, precise)
    out = _sigmoid(ga_ref[...].astype(F32)) * ya + _sigmoid(gb_ref[...].astype(F32)) * ym
    o_ref[...] = out.astype(o_ref.dtype)


def merge(meta, att, hm, w_att_out, w_mlstm_out, tail, out_dtype, precise, tn=512):
    T = att.shape[0]
    tm = _m_tile(T)
    ga_off = M_WIDTH // tn
    gb_off = (M_WIDTH + D_MODEL) // tn
    return _pcall(
        functools.partial(_merge_body, precise=precise), (T // tm, D_MODEL // tn),
        [pl.BlockSpec((tm, ATT_OUT), lambda i, j, m: (i, 0)),
         pl.BlockSpec((tm, M_WIDTH), lambda i, j, m: (i, 0)),
         pl.BlockSpec((None, ATT_OUT, tn), lambda i, j, m: (m[0], 0, j)),
         pl.BlockSpec((None, M_WIDTH, tn), lambda i, j, m: (m[0], 0, j)),
         pl.BlockSpec((tm, tn), lambda i, j, m: (i, ga_off + j)),
         pl.BlockSpec((tm, tn), lambda i, j, m: (i, gb_off + j))],
        pl.BlockSpec((tm, tn), lambda i, j, m: (i, j)),
        jax.ShapeDtypeStruct((T, D_MODEL), out_dtype),
        sem=("parallel", "arbitrary"), name="merge")(meta, att, hm, w_att_out, w_mlstm_out, tail, tail)


def _top2_sum(a, b, c, d):
    hi1, lo1 = jnp.maximum(a, b), jnp.minimum(a, b)
    hi2, lo2 = jnp.maximum(c, d), jnp.minimum(c, d)
    return jnp.maximum(hi1, hi2) + jnp.maximum(jnp.minimum(hi1, hi2), jnp.maximum(lo1, lo2))


def _route_rows(s, sb):
    rows = [sb[e:e + 1, :] for e in range(N_EXPERTS)]
    urows = [s[e:e + 1, :] for e in range(N_EXPERTS)]
    gs = [_top2_sum(*rows[g * EPG:(g + 1) * EPG]) for g in range(N_EGROUPS)]
    best = gs[0]
    gsel = jnp.zeros(best.shape, I32)
    for g in range(1, N_EGROUPS):
        upd = gs[g] > best
        gsel = jnp.where(upd, g, gsel)
        best = jnp.where(upd, gs[g], best)
    vals, uvals = [], []
    for i in range(EPG):
        v, u = rows[i], urows[i]
        for g in range(1, N_EGROUPS):
            v = jnp.where(gsel == g, rows[g * EPG + i], v)
            u = jnp.where(gsel == g, urows[g * EPG + i], u)
        vals.append(v)
        uvals.append(u)
    b1, i1, u1 = vals[0], jnp.zeros(best.shape, I32), uvals[0]
    for i in range(1, EPG):
        upd = vals[i] > b1
        i1 = jnp.where(upd, i, i1)
        u1 = jnp.where(upd, uvals[i], u1)
        b1 = jnp.where(upd, vals[i], b1)
    b2 = jnp.full(best.shape, -jnp.inf, F32)
    i2 = jnp.zeros(best.shape, I32)
    u2 = jnp.zeros(best.shape, F32)
    for i in range(EPG):
        upd = (i1 != i) & (vals[i] > b2)
        i2 = jnp.where(upd, i, i2)
        u2 = jnp.where(upd, uvals[i], u2)
        b2 = jnp.where(upd, vals[i], b2)
    tot = u1 + u2
    eidx = jnp.concatenate([gsel * EPG + i1, gsel * EPG + i2], axis=0)
    wts = jnp.concatenate([u1 / tot, u2 / tot], axis=0)
    return eidx, wts


def _norm_route_body(meta, x_ref, g_ref, sc_ref, sh_ref, wr_ref, br_ref, hf_ref, e_ref, w_ref, *, on_mxu):
    hf = _rms(x_ref[...], g_ref[...]) * (1.0 + sc_ref[...]) + sh_ref[...]
    hf_ref[...] = hf.astype(hf_ref.dtype)
    wr = wr_ref[...]
    if on_mxu:
        h_hi, h_lo = _split(hf)
        w_hi, w_lo = _split(wr)
        logits = _dot_nt(w_hi, h_hi) + (_dot_nt(w_hi, h_lo) + _dot_nt(w_lo, h_hi))
    else:
        logits = jnp.sum(wr * hf, axis=1, keepdims=True)
    s = _sigmoid(logits)
    eidx, wts = _route_rows(s, s + br_ref[...])
    e_ref[...] = eidx
    w_ref[...] = wts


def norm_route(meta, x3, g_all, sc, sh, wr_t, b_router, hf_dtype):
    B, S, D = x3.shape
    ts = _row_tile(S)
    hf, eidx, wts = _pcall(
        functools.partial(_norm_route_body, on_mxu=ts >= LANES), (B, S // ts),
        [pl.BlockSpec((None, ts, D), lambda b, i, m: (b, i, 0)),
         pl.BlockSpec((None, 1, D), lambda b, i, m: (m[0], 0, 0)),
         pl.BlockSpec((None, 1, D), lambda b, i, m: (b, 0, 0)),
         pl.BlockSpec((None, 1, D), lambda b, i, m: (b, 0, 0)),
         pl.BlockSpec((N_EXPERTS, D), lambda b, i, m: (0, 0)),
         pl.BlockSpec((N_EXPERTS, 1), lambda b, i, m: (0, 0))],
        [pl.BlockSpec((None, ts, D), lambda b, i, m: (b, i, 0)),
         pl.BlockSpec((None, TOP_K, ts), lambda b, i, m: (b, 0, i)),
         pl.BlockSpec((None, TOP_K, ts), lambda b, i, m: (b, 0, i))],
        [jax.ShapeDtypeStruct((B, S, D), hf_dtype),
         jax.ShapeDtypeStruct((B, TOP_K, S), I32),
         jax.ShapeDtypeStruct((B, TOP_K, S), F32)],
        sem=("parallel", "parallel"), name="norm_route")(
            meta, x3, g_all, sc, sh, wr_t, b_router.reshape(N_EXPERTS, 1))
    T = B * S
    eidx = jnp.swapaxes(eidx, 1, 2).reshape(T, TOP_K)
    wts = jnp.swapaxes(wts, 1, 2).reshape(T, TOP_K)
    return hf.reshape(T, D), eidx, wts


def _expert_body(meta, x_ref, rw_ref, wg_ref, wu_ref, wd_ref, o_ref, *scratch, precise):
    blk = pl.program_id(0)
    n_used = meta[1]
    e = meta[2 + blk]
    e_prev = meta[2 + jnp.maximum(blk - 1, 0)]
    refs = (wg_ref, wu_ref, wd_ref)

    @pl.when((blk == 0) | (e != e_prev))
    def _():
        for i, r in enumerate(refs):
            if precise:
                hi, lo = _split(r[...])
                scratch[2 * i][...] = hi
                scratch[2 * i + 1][...] = lo
            else:
                scratch[i][...] = r[...].astype(BF16)

    @pl.when(blk < n_used)
    def _():
        if precise:
            x_hi, x_lo = _split(x_ref[...])
            h1 = _dot3(x_hi, x_lo, scratch[0][...], scratch[1][...])
            h2 = _dot3(x_hi, x_lo, scratch[2][...], scratch[3][...])
            a_hi, a_lo = _split(h1 * _sigmoid(h1) * h2)
            y = _dot3(a_hi, a_lo, scratch[4][...], scratch[5][...])
        else:
            x = x_ref[...]
            h1 = _dot(x, scratch[0][...])
            h2 = _dot(x, scratch[1][...])
            y = _dot((h1 * _sigmoid(h1) * h2).astype(BF16), scratch[2][...])
        o_ref[...] = y * rw_ref[...]

    @pl.when(blk >= n_used)
    def _():
        o_ref[...] = jnp.zeros_like(o_ref)


def _expert_gather_body(meta, tokn_ref, tok0_ref, dstp_ref, wg_ref, wu_ref, wd_ref, hf_hbm, y_hbm,
                        wg_s, wu_s, wd_s, xbuf, ybuf, sem_in, sem_out, *, bm):
    blk = pl.program_id(0)
    n_used = meta[1]
    e = meta[2 + blk]
    e_prev = meta[2 + jnp.maximum(blk - 1, 0)]
    cur = lax.rem(blk, 2)
    oth = 1 - cur

    def gather(tok_ref, i, slot):
        return pltpu.make_async_copy(hf_hbm.at[pl.ds(tok_ref[0, i], 1), :], xbuf.at[slot, pl.ds(i, 1), :],
                                     sem_in.at[slot])

    def scatter(i, slot):
        return pltpu.make_async_copy(ybuf.at[slot, pl.ds(i, 1), :], y_hbm.at[pl.ds(dstp_ref[0, i], 1), :],
                                     sem_out.at[slot])

    @pl.when((blk == 0) | (e != e_prev))
    def _():
        wg_s[...] = wg_ref[...].astype(BF16)
        wu_s[...] = wu_ref[...].astype(BF16)
        wd_s[...] = wd_ref[...].astype(BF16)

    @pl.when(blk == 0)
    def _():
        ybuf[1] = jnp.zeros((bm, ybuf.shape[2]), F32)
        for i in range(bm):
            gather(tok0_ref, i, 0).start(priority=i % 2)

    @pl.when(blk <= n_used)
    def _():
        for i in range(bm):
            gather(tokn_ref, i, cur).wait()

    @pl.when((blk >= 1) & (blk <= n_used))
    def _():
        for i in range(bm):
            scatter(i, cur).wait()

    @pl.when(blk < n_used)
    def _():
        x = xbuf[cur].astype(BF16)
        h1 = _dot(x, wg_s[...])
        for i in range(bm):
            gather(tokn_ref, i, oth).start(priority=i % 2)
        h2 = _dot(x, wu_s[...])
        for i in range(bm):
            scatter(i, oth).start(priority=i % 2)
        ybuf[cur] = _dot((h1 * _sigmoid(h1) * h2).astype(BF16), wd_s[...])

    @pl.when(blk == n_used)
    def _():
        for i in range(bm):
            scatter(i, oth).start(priority=i % 2)
        for i in range(bm):
            scatter(i, oth).wait()


def moe(layer, hf, eidx, wts, w_gate, w_up, w_down, bm, precise):
    T, D = hf.shape
    A = T * TOP_K
    n_blocks = -(-A // bm) + N_EXPERTS
    rows = n_blocks * bm
    e_flat = eidx.reshape(A)
    onehot = (e_flat[:, None] == jnp.arange(N_EXPERTS, dtype=I32)[None, :]).astype(I32)
    csum = jnp.cumsum(onehot, axis=0)
    counts = csum[-1]
    rank = jnp.sum((csum - onehot) * onehot, axis=1)
    padded = (counts + bm - 1) // bm * bm
    pad_end = jnp.cumsum(padded)
    pad_start = pad_end - padded
    pos = jnp.sum(pad_start[None, :] * onehot, axis=1) + rank
    n_used = pad_end[-1] // bm
    blk_ids = jnp.arange(n_blocks, dtype=I32)
    blk_exp = jnp.minimum(jnp.sum((pad_end[None, :] <= (blk_ids * bm)[:, None]).astype(I32), axis=1), N_EXPERTS - 1)
    last_exp = blk_exp[jnp.maximum(n_used - 1, 0)]
    blk_exp = jnp.where(blk_ids < n_used, blk_exp, last_exp)
    meta = jnp.concatenate([layer.reshape(1), n_used.reshape(1).astype(I32), blk_exp])
    n_scr = 6 if precise else 3
    scr = []
    for shape in ((D, D_EXPERT), (D, D_EXPERT), (D_EXPERT, D)):
        scr += [pltpu.VMEM(shape, BF16)] * (n_scr // 3)
    a_ids = jnp.arange(A, dtype=I32)
    if not precise:
        y_rows = TOP_K * T + bm
        spare = TOP_K * T + jnp.arange(rows, dtype=I32) % bm
        base = jnp.stack([jnp.zeros((rows,), I32), spare], axis=1)
        vals = jnp.stack([a_ids // TOP_K, (a_ids % TOP_K) * T + a_ids // TOP_K], axis=1)
        tab = base.at[pos].set(vals)
        tok_tab = tab[:, 0].reshape(n_blocks, 1, bm)
        dst_tab = jnp.concatenate([spare[:bm], tab[:, 1]]).reshape(n_blocks + 1, 1, bm)
        smem = functools.partial(pl.BlockSpec, memory_space=pltpu.SMEM)
        return _pcall(
            functools.partial(_expert_gather_body, bm=bm), (n_blocks,),
            [smem((None, 1, bm), lambda b, m: (jnp.minimum(b + 1, n_blocks - 1), 0, 0)),
             smem((None, 1, bm), lambda b, m: (0, 0, 0)),
             smem((None, 1, bm), lambda b, m: (b, 0, 0)),
             pl.BlockSpec((None, None, D, D_EXPERT), lambda b, m: (m[0], m[2 + b], 0, 0)),
             pl.BlockSpec((None, None, D, D_EXPERT), lambda b, m: (m[0], m[2 + b], 0, 0)),
             pl.BlockSpec((None, None, D_EXPERT, D), lambda b, m: (m[0], m[2 + b], 0, 0)),
             pl.BlockSpec(memory_space=pl.ANY)],
            pl.BlockSpec(memory_space=pl.ANY),
            jax.ShapeDtypeStruct((y_rows, D), F32),
            scratch=scr + [pltpu.VMEM((2, bm, D), F32), pltpu.VMEM((2, bm, D), F32),
                           pltpu.SemaphoreType.DMA((2,)), pltpu.SemaphoreType.DMA((2,))],
            sem=("arbitrary",), name="experts_gather")(
                meta, tok_tab, tok_tab, dst_tab, w_gate, w_up, w_down, hf)
    row_tok = jnp.full((rows,), T, I32).at[pos].set(a_ids // TOP_K)
    row_w = jnp.zeros((rows,), F32).at[pos].set(wts.reshape(A))
    xs = jnp.concatenate([hf, jnp.zeros((1, D), hf.dtype)], axis=0)[row_tok]
    ys = _pcall(
        functools.partial(_expert_body, precise=precise), (n_blocks,),
        [pl.BlockSpec((bm, D), lambda b, m: (b, 0)),
         pl.BlockSpec((bm, 1), lambda b, m: (b, 0)),
         pl.BlockSpec((None, None, D, D_EXPERT), lambda b, m: (m[0], m[2 + b], 0, 0)),
         pl.BlockSpec((None, None, D, D_EXPERT), lambda b, m: (m[0], m[2 + b], 0, 0)),
         pl.BlockSpec((None, None, D_EXPERT, D), lambda b, m: (m[0], m[2 + b], 0, 0))],
        pl.BlockSpec((bm, D), lambda b, m: (b, 0)),
        jax.ShapeDtypeStruct((rows, D), F32),
        scratch=scr, sem=("arbitrary",), name="experts")(
            meta, xs, row_w.reshape(rows, 1), w_gate, w_up, w_down)
    return ys[pos.reshape(T, TOP_K).T.reshape(A)].reshape(TOP_K, T, D)


def _moe_residual_body(meta, x_ref, y0_ref, y1_ref, gt_ref, o_ref):
    o_ref[...] = x_ref[...] + gt_ref[...] * (y0_ref[...] + y1_ref[...])


def _moe_residual_w_body(meta, x_ref, y0_ref, y1_ref, gt_ref, w_ref, o_ref):
    w = w_ref[...]
    o_ref[...] = x_ref[...] + gt_ref[...] * (y0_ref[...] * w[:, 0:1] + y1_ref[...] * w[:, 1:2])


def moe_residual(x3, y2, gt, wts=None):
    B, S, D = x3.shape
    ts = _row_tile(S)
    meta = jnp.zeros((1,), I32)
    x_spec = pl.BlockSpec((None, ts, D), lambda b, i, m: (b, i, 0))
    gt_spec = pl.BlockSpec((None, 1, D), lambda b, i, m: (b, 0, 0))
    if y2.ndim == 2:
        per = S // ts
        body = _moe_residual_w_body
        in_specs = [x_spec,
                    pl.BlockSpec((ts, D), lambda b, i, m: (b * per + i, 0)),
                    pl.BlockSpec((ts, D), lambda b, i, m: (B * per + b * per + i, 0)),
                    gt_spec,
                    pl.BlockSpec((None, ts, TOP_K), lambda b, i, m: (b, i, 0))]
        args = (x3, y2, y2, gt, wts.reshape(B, S, TOP_K))
    else:
        y3 = y2.reshape(TOP_K, B, S, D)
        body = _moe_residual_body
        in_specs = [x_spec,
                    pl.BlockSpec((None, None, ts, D), lambda b, i, m: (0, b, i, 0)),
                    pl.BlockSpec((None, None, ts, D), lambda b, i, m: (1, b, i, 0)),
                    gt_spec]
        args = (x3, y3, y3, gt)
    return _pcall(
        body, (B, S // ts), in_specs, x_spec, jax.ShapeDtypeStruct((B, S, D), F32),
        sem=("parallel", "parallel"), name="moe_residual")(meta, *args)


def _rope_tables(pos):
    half = HEAD_DIM // 2
    inv = ROPE_THETA ** (-jnp.arange(half, dtype=F32) / half)
    ang = pos.astype(F32)[:, None] * inv[None, :]
    cos, sin = jnp.cos(ang), jnp.sin(ang)
    return jnp.concatenate([cos, cos], axis=1), jnp.concatenate([-sin, sin], axis=1)


def _layer(layer, x3, mod, params, rope, sample_state):
    (g_mix, g_ffn, w_in, w_gates, w_tail, b_igate, b_fgate, g_mlstm_out, w_att_out, w_mlstm_out,
     w_mix_out, wr_t, b_router, w_exp_gate, w_exp_up, w_exp_down) = params
    B, S, D = x3.shape
    T = B * S
    sample = sample_state is not None
    act = F32 if sample else BF16
    meta = layer.reshape(1)
    sh1, sc1, gt1, sh2, sc2, gt2 = [m.reshape(B, 1, D) for m in jnp.split(mod, 6, axis=-1)]

    wins = []
    if sample:
        h = norm_mod(meta, x3, g_mix, sc1, sh1, act).reshape(T, D)
        proj = linear(meta, h, w_in, 0, GATE_OFF, F32, True, rot=(rope[0], rope[1], 2 * ATT_WIDTH // 512))
        qkv, qkvm = proj[:, :3 * ATT_WIDTH], proj[:, 3 * ATT_WIDTH:]
        for g in range(len(ATT_GROUPS)):
            kg = qkv[:, ATT_WIDTH + g * ATT_OUT:ATT_WIDTH + (g + 1) * ATT_OUT]
            vg = qkv[:, 2 * ATT_WIDTH + g * ATT_OUT:2 * ATT_WIDTH + (g + 1) * ATT_OUT]
            wins.append(jnp.concatenate([kg, vg], axis=1).reshape(B, S, 2, ATT_HPG, HEAD_DIM))
    else:
        hs = norm_mod_streams(meta, x3, g_mix, sc1, sh1)
        h = hs[0]
        qkv_g = [linear(meta, hs[g], w_in, g * ATT_OUT, 3 * ATT_OUT, F32, False, rot=(rope[g][0], rope[g][1], 2),
                        col_step=len(ATT_GROUPS)) for g in range(len(ATT_GROUPS))]
        for g, (win, dil) in enumerate(ATT_GROUPS):
            keep = min(win, S)
            L = S // dil
            kv = qkv_g[g].reshape(B, dil, L, 3 * ATT_OUT)[:, :, L - keep // dil:, ATT_OUT:]
            wins.append(jnp.swapaxes(kv, 1, 2).reshape(B, keep, 2, ATT_HPG, HEAD_DIM))
    if not sample:
        qkvm = linear(meta, h, w_in, 3 * ATT_WIDTH, 3 * M_WIDTH, act, False)
    gates = linear(meta, h, w_gates, 0, LANES, F32, sample, tn=LANES)
    tail = linear(meta, h, w_tail, 0, M_WIDTH + 2 * D_MODEL, act, sample)

    if sample:
        caches, st_c, st_n, st_m = sample_state
        att = sample_attention(meta, qkv, caches)
        hm, c1, n1, m1 = mlstm_sample(meta, qkvm, gates, b_igate, b_fgate, tail, g_mlstm_out, st_c, st_n, st_m)
    else:
        outs, lses = zip(*[band_attention(qkv_g[g], B, S, g) for g in range(len(ATT_GROUPS))])
        att = attn_merge(outs, lses)
        hm, c1, n1, m1 = mlstm_prompt(meta, qkvm, gates, b_igate, b_fgate, tail, g_mlstm_out, B, S)

    merged = merge(meta, att, hm, w_att_out, w_mlstm_out, tail, act, sample)
    x2 = linear_residual(meta, merged, w_mix_out, x3.reshape(T, D), gt1, S, sample)
    x3 = x2.reshape(B, S, D)

    hf, eidx, wts = norm_route(meta, x3, g_ffn, sc2, sh2, wr_t, b_router, F32)
    bm = MOE_BLOCK_SAMPLE if sample else MOE_BLOCK_PROMPT
    y2 = moe(layer, hf, eidx, wts, w_exp_gate, w_exp_up, w_exp_down, bm, sample)
    x3 = moe_residual(x3, y2, gt2, None if sample else wts)
    return x3, (wins[0], wins[1], wins[2], c1, n1, m1)


def kernel(x_prompt, x_sample, cache_win_w128, cache_win_w512, cache_win_w2048, state_mlstm_c, state_mlstm_n,
           state_mlstm_m, c_prompt, c_sample, w_ada, b_ada, g_mix, g_ffn, w_in, b_igate, b_fgate, g_mlstm_out,
           w_att_out, w_mlstm_out, w_mix_out, w_router, b_router, w_exp_gate, w_exp_up, w_exp_down, g_final):
    B, S, D = x_prompt.shape
    Bd, Sd, _ = x_sample.shape
    assert Sd == 1 and D == D_MODEL and w_in.shape[2] == D_IN

    n_c = B + Bd
    c_rows = -(-n_c // SUBLANES) * SUBLANES
    c_all = jnp.pad(jnp.concatenate([c_prompt, c_sample], axis=0), ((0, c_rows - n_c), (0, 0)))
    mod = ada_modulation(c_all, w_ada, b_ada)

    w_gates = jnp.pad(w_in[:, :, GATE_OFF:TAIL_OFF], ((0, 0), (0, 0), (0, LANES - 2 * M_HEADS)))
    w_tail = w_in[:, :, TAIL_OFF:]
    params = (g_mix.reshape(DEPTH, 1, D), g_ffn.reshape(DEPTH, 1, D), w_in, w_gates, w_tail, b_igate, b_fgate,
              g_mlstm_out, w_att_out, w_mlstm_out, w_mix_out, w_router.T, b_router,
              w_exp_gate, w_exp_up, w_exp_down)
    rope_p = [_rope_tables((jnp.arange(S // dil, dtype=I32)[None, :] * dil
                            + jnp.arange(dil, dtype=I32)[:, None]).reshape(S)) for _, dil in ATT_GROUPS]
    rope_s = _rope_tables(jnp.full((Bd,), PAST_LEN, I32))
    caches = []
    for c, (win, dil) in zip((cache_win_w128, cache_win_w512, cache_win_w2048), ATT_GROUPS):
        assert c.shape[2] == win and win == ATT_BLOCK * dil
        caches.append(c[:, :, ::dil].reshape(DEPTH, Bd, ATT_BLOCK, 2 * ATT_OUT))
    sample_state = (caches, state_mlstm_c, state_mlstm_n, state_mlstm_m)

    xp, xs = x_prompt, x_sample
    outs_p, outs_s = [], []
    for l in range(DEPTH):
        layer = jnp.asarray(l, I32)
        xp, out_p = _layer(layer, xp, mod[l, :B], params, rope_p, None)
        xs, out_s = _layer(layer, xs, mod[l, B:n_c], params, rope_s, sample_state)
        outs_p.append(out_p)
        outs_s.append(out_s)
    y_prompt = norm_final(xp, g_final)
    y_sample = norm_final(xs, g_final)
    stack = lambda outs: tuple(jnp.stack(leaves, axis=0) for leaves in zip(*outs))
    return (y_prompt, y_sample) + stack(outs_p) + stack(outs_s)
```
